```python
import math
import jax, jax.numpy as jnp
from jax import lax
import numpy as np

D_MODEL = 1024
BATCH = 2
SEQ = 16384
DEPTH = 2

N_A = DEPTH // 2
N_B = DEPTH - N_A
POOL_WINDOWS = (2, 4, 8, 16)
N_POOL_GROUPS = len(POOL_WINDOWS)
POOL_GROUP_DIM = D_MODEL // N_POOL_GROUPS
HEAD_DIM = 64
N_HEADS = D_MODEL // HEAD_DIM
BLOCK_Q = 128
N_GROUPS = 4
EXPERTS_PER_GROUP = 8
N_EXPERTS = N_GROUPS * EXPERTS_PER_GROUP
TOP_K = 2
D_EXPERT = D_MODEL // 4
PLE_DIM = 256
RMS_EPS = 1e-6

kernel_name = "yoco_pool_stickbreaking_hmoe_ple"


def rmsnorm(x, g):
    xf = x.astype(jnp.float32)
    y = xf * lax.rsqrt(jnp.mean(xf * xf, axis=-1, keepdims=True) + RMS_EPS)
    return (y * g.astype(jnp.float32)).astype(x.dtype)


def causal_window_mean(xg, w):
    S = xg.shape[1]
    c0 = jnp.pad(lax.cumsum(xg, axis=1), ((0, 0), (1, 0), (0, 0)))
    c_lag = jnp.pad(c0, ((0, 0), (w - 1, 0), (0, 0)))[:, :S]
    count = jnp.minimum(jnp.arange(1, S + 1), w).astype(jnp.float32)
    return (c0[:, 1:] - c_lag) / count[None, :, None]


def multiscale_pool_mixer(xn, w_pool, b_pool, ls_pool):
    B, S, D = xn.shape
    xf = xn.astype(jnp.float32).reshape(B, S, N_POOL_GROUPS, POOL_GROUP_DIM)
    pooled = jnp.stack(
        [causal_window_mean(xf[:, :, g], w) - xf[:, :, g] for g, w in enumerate(POOL_WINDOWS)],
        axis=2)
    y = jnp.einsum('bsgc,gcd->bsgd', pooled.astype(xn.dtype), w_pool) + b_pool
    return y.reshape(B, S, D) * ls_pool


def stick_breaking_attention(q, k, v):
    S = q.shape[2]
    scale = 1.0 / math.sqrt(HEAD_DIM)
    outs = []
    for i in range(S // BLOCK_Q):
        q0 = i * BLOCK_Q
        end = q0 + BLOCK_Q
        qb = q[:, :, q0:end]
        kb = k[:, :, :end]
        vb = v[:, :, :end]
        z = jnp.einsum('bhtd,bhsd->bhts', qb, kb).astype(jnp.float32) * scale
        t_pos = q0 + jnp.arange(BLOCK_Q)[:, None]
        s_pos = jnp.arange(end)[None, :]
        causal = s_pos < t_pos
        neg_log_1mb = jnp.where(causal, jax.nn.softplus(z), 0.0)
        suffix = lax.cumsum(neg_log_1mb, axis=3, reverse=True) - neg_log_1mb
        a = jnp.where(causal, jnp.exp(jax.nn.log_sigmoid(z) - suffix), 0.0)
        outs.append(jnp.einsum('bhts,bhsd->bhtd', a.astype(vb.dtype), vb))
    return jnp.concatenate(outs, axis=2)


def split_heads(t):
    B, S, _ = t.shape
    return t.reshape(B, S, N_HEADS, HEAD_DIM).transpose(0, 2, 1, 3)


def hierarchical_moe(xn, w_router_group, w_router_expert, w_gate_up, w_down):
    B, S, D = xn.shape
    xt = xn.reshape(-1, D)
    N = xt.shape[0]
    g_logits = (xt @ w_router_group).astype(jnp.float32)
    g_prob = jax.nn.softmax(g_logits, axis=-1)
    g_top_p, g_top_i = lax.top_k(g_prob, 1)
    e_logits = (xt @ w_router_expert).astype(jnp.float32).reshape(N, N_GROUPS, EXPERTS_PER_GROUP)
    g_onehot = jax.nn.one_hot(g_top_i[:, 0], N_GROUPS, dtype=jnp.float32)
    e_sel = jnp.sum(e_logits * g_onehot[:, :, None], axis=1)
    top_v, top_i = lax.top_k(e_sel, TOP_K)
    top_w = jax.nn.softmax(top_v, axis=-1) * g_top_p
    expert_id = g_top_i * EXPERTS_PER_GROUP + top_i
    combine = jnp.sum(jax.nn.one_hot(expert_id, N_EXPERTS, dtype=jnp.float32)
                      * top_w[..., None], axis=1).astype(xn.dtype)
    out = jnp.zeros_like(xt)
    for e in range(N_EXPERTS):
        gu = xt @ w_gate_up[e]
        h = jax.nn.silu(gu[:, :D_EXPERT]) * gu[:, D_EXPERT:]
        out = out + combine[:, e:e + 1] * (h @ w_down[e])
    return out.reshape(B, S, D)


def per_layer_embedding(h, p_i, g_ple, w_ple_gate, w_ple_proj):
    gate = jax.nn.sigmoid(rmsnorm(h, g_ple) @ w_ple_gate)
    return (p_i @ w_ple_proj) * gate


def setup_inputs(seed: int = 0) -> dict:
    key = jax.random.key(seed)
    ks = jax.random.split(key, 24)
    f32 = jnp.float32
    nrm = lambda k, shape, fan_in: jax.random.normal(k, shape, f32) * (fan_in ** -0.5)
    gain = lambda k, shape: 1.0 + 0.05 * jax.random.normal(k, shape, f32)
    D = D_MODEL
    return {
        "x": jax.random.normal(ks[0], (BATCH, SEQ, D), f32),
        "p": jax.random.normal(ks[1], (DEPTH, BATCH, SEQ, PLE_DIM), f32),
        "g_mix_a": gain(ks[2], (N_A, D)),
        "w_pool": nrm(ks[3], (N_A, N_POOL_GROUPS, POOL_GROUP_DIM, POOL_GROUP_DIM), POOL_GROUP_DIM),
        "b_pool": 0.01 * jax.random.normal(ks[4], (N_A, N_POOL_GROUPS, POOL_GROUP_DIM), f32),
        "ls_pool": 0.5 + 0.05 * jax.random.normal(ks[5], (N_A, D), f32),
        "g_kv": gain(ks[6], (D,)),
        "w_kv": nrm(ks[7], (D, 2 * D), D),
        "g_mix_b": gain(ks[8], (N_B, D)),
        "w_q": nrm(ks[9], (N_B, D, D), D),
        "w_o": nrm(ks[10], (N_B, D, D), D),
        "g_moe": gain(ks[11], (DEPTH, D)),
        "w_router_group": nrm(ks[12], (DEPTH, D, N_GROUPS), D),
        "w_router_expert": nrm(ks[13], (DEPTH, D, N_GROUPS * EXPERTS_PER_GROUP), D),
        "w_gate_up": nrm(ks[14], (DEPTH, N_EXPERTS, D, 2 * D_EXPERT), D),
        "w_down": nrm(ks[15], (DEPTH, N_EXPERTS, D_EXPERT, D), D_EXPERT),
        "g_ple": gain(ks[16], (DEPTH, D)),
        "w_ple_gate": nrm(ks[17], (DEPTH, D, D), D),
        "w_ple_proj": nrm(ks[18], (DEPTH, PLE_DIM, D), PLE_DIM),
        "g_final": gain(ks[19], (D,)),
    }


def reference(x, p, g_mix_a, w_pool, b_pool, ls_pool, g_kv, w_kv, g_mix_b, w_q, w_o,
              g_moe, w_router_group, w_router_expert, w_gate_up, w_down,
              g_ple, w_ple_gate, w_ple_proj, g_final):
    h = x
    k_sh = None
    v_sh = None
    for layer in range(DEPTH):
        if layer < N_A:
            a = layer
            h = h + multiscale_pool_mixer(rmsnorm(h, g_mix_a[a]), w_pool[a], b_pool[a], ls_pool[a])
        else:
            b = layer - N_A
            q = split_heads(rmsnorm(h, g_mix_b[b]) @ w_q[b])
            o = stick_breaking_attention(q, k_sh, v_sh)
            o = o.transpose(0, 2, 1, 3).reshape(h.shape)
            h = h + o @ w_o[b]
        h = h + hierarchical_moe(rmsnorm(h, g_moe[layer]), w_router_group[layer],
                                 w_router_expert[layer], w_gate_up[layer], w_down[layer])
        h = h + per_layer_embedding(h, p[layer], g_ple[layer], w_ple_gate[layer], w_ple_proj[layer])
        if layer == N_A - 1:
            kv = rmsnorm(h, g_kv) @ w_kv
            k_sh = split_heads(kv[..., :D_MODEL])
            v_sh = split_heads(kv[..., D_MODEL:])
    return rmsnorm(h, g_final)
```

```python
import functools
import math

import jax
import jax.numpy as jnp
from jax import lax
from jax.experimental import pallas as pl
from jax.experimental.pallas import tpu as pltpu

D_MODEL = 1024
POOL_WINDOWS = (2, 4, 8, 16)
POOL_GROUP_DIM = D_MODEL // len(POOL_WINDOWS)
POOL_HALO = 16
HEAD_DIM = 64
N_GROUPS = 4
EXPERTS_PER_GROUP = 8
N_EXPERTS = N_GROUPS * EXPERTS_PER_GROUP
D_EXPERT = D_MODEL // 4
PLE_DIM = 256
RMS_EPS = 1e-6
LANES = 128
ROUTER_LANES = LANES
SUFFIX_CUTOFF = 106.0

F32 = jnp.float32
BF16 = jnp.bfloat16


def _vmem_limit(mib):
    return pltpu.CompilerParams(vmem_limit_bytes=mib * 1024 * 1024)


def _rms_scale(x):
    return lax.rsqrt(jnp.mean(x * x, axis=-1, keepdims=True) + RMS_EPS)


def _split_bf16(x):
    hi = x.astype(BF16)
    lo = (x - hi.astype(F32)).astype(BF16)
    return hi, lo


def _dot(a, b):
    return jnp.dot(a, b, preferred_element_type=F32)


def _pool_kernel(x_ref, xprev_ref, g_ref, w_ref, b_ref, ls_ref, o_ref, *, ts):
    i = pl.program_id(1)
    xc = x_ref[0]
    xp = xprev_ref[0]
    g = g_ref[...]
    xn_c = xc * _rms_scale(xc) * g
    xn_p = xp * _rms_scale(xp) * g * (i > 0).astype(F32)
    full = jnp.concatenate([xn_p, xn_c], axis=0)
    t_glob = i * ts + lax.broadcasted_iota(jnp.int32, (ts, 1), 0)
    ys = []
    for gi, w in enumerate(POOL_WINDOWS):
        f = full[:, gi * POOL_GROUP_DIM:(gi + 1) * POOL_GROUP_DIM]
        s = f
        k = 1
        while k < w:
            s = s + pltpu.roll(s, k, axis=0)
            k *= 2
        cnt = jnp.minimum(t_glob + 1, w).astype(F32)
        pooled = s[POOL_HALO:] / cnt - f[POOL_HALO:]
        ys.append(_dot(pooled.astype(BF16), w_ref[gi]))
    y = jnp.concatenate(ys, axis=1) + b_ref[...]
    o_ref[0] = xc + y * ls_ref[...]


def _pool_mixer(x, g, w_pool, b_pool, ls_pool, *, ts=512):
    B, S, D = x.shape
    row = lambda v: v.reshape(1, D)
    const = lambda shape: pl.BlockSpec(shape, lambda b, i: (0,) * len(shape))
    return pl.pallas_call(
        functools.partial(_pool_kernel, ts=ts),
        grid=(B, S // ts),
        in_specs=[
            pl.BlockSpec((1, ts, D), lambda b, i: (b, i, 0)),
            pl.BlockSpec((1, POOL_HALO, D),
                         lambda b, i: (b, jnp.maximum(i * (ts // POOL_HALO) - 1, 0), 0)),
            const((1, D)),
            const(w_pool.shape),
            const((1, D)),
            const((1, D)),
        ],
        out_specs=pl.BlockSpec((1, ts, D), lambda b, i: (b, i, 0)),
        out_shape=jax.ShapeDtypeStruct((B, S, D), F32),
        compiler_params=_vmem_limit(40),
        name="pool_mixer",
    )(x, x, row(g), w_pool.astype(BF16), row(b_pool), row(ls_pool))


def _route(logits):
    t = logits.shape[0]
    lane = lax.broadcasted_iota(jnp.int32, (t, ROUTER_LANES), 1).astype(F32)
    neg = -jnp.inf
    big = float(ROUTER_LANES)
    gl = jnp.where(lane < N_GROUPS, logits, neg)
    gmax = jnp.max(gl, axis=1, keepdims=True)
    g_top_p = 1.0 / jnp.sum(jnp.exp(gl - gmax), axis=1, keepdims=True)
    g_idx = jnp.min(jnp.where(gl == gmax, lane, big), axis=1, keepdims=True)
    lo = N_GROUPS + EXPERTS_PER_GROUP * g_idx
    el = jnp.where((lane >= lo) & (lane < lo + EXPERTS_PER_GROUP), logits, neg)
    v1 = jnp.max(el, axis=1, keepdims=True)
    i1 = jnp.min(jnp.where(el == v1, lane, big), axis=1, keepdims=True)
    el2 = jnp.where(lane == i1, neg, el)
    v2 = jnp.max(el2, axis=1, keepdims=True)
    i2 = jnp.min(jnp.where(el2 == v2, lane, big), axis=1, keepdims=True)
    ex = jnp.exp(v2 - v1)
    w1 = g_top_p / (1.0 + ex)
    w2 = g_top_p * ex / (1.0 + ex)
    return jnp.where(lane == i1, w1, 0.0) + jnp.where(lane == i2, w2, 0.0)


def _moe_kernel(h_ref, g_ref, wrh_ref, wrl_ref, wgu_ref, wd_ref, o_ref, xn_ref, comb_ref):
    e = pl.program_id(1)

    @pl.when(e == 0)
    def _():
        x = h_ref[...]
        xn = x * _rms_scale(x) * g_ref[...]
        xh, xl = _split_bf16(xn)
        wrh = wrh_ref[...]
        logits = _dot(xh, wrh) + _dot(xl, wrh) + _dot(xh, wrl_ref[...])
        comb_ref[...] = _route(logits)
        xn_ref[...] = xh
        o_ref[...] = x

    gu = _dot(xn_ref[...], wgu_ref[0])
    gate = gu[:, :D_EXPERT]
    hmid = gate * jax.nn.sigmoid(gate) * gu[:, D_EXPERT:]
    y = _dot(hmid.astype(BF16), wd_ref[0])
    lane = lax.broadcasted_iota(jnp.int32, (1, ROUTER_LANES), 1)
    c = jnp.sum(jnp.where(lane == e + N_GROUPS, comb_ref[...], 0.0), axis=1, keepdims=True)
    o_ref[...] += c * y


def _moe(h, g, w_rg, w_re, w_gu, w_d, *, tm=1024):
    N, D = h.shape
    w_r = jnp.concatenate([w_rg, w_re], axis=1)
    w_r = jnp.pad(w_r, ((0, 0), (0, ROUTER_LANES - w_r.shape[1])))
    wrh, wrl = _split_bf16(w_r)
    return pl.pallas_call(
        _moe_kernel,
        grid=(N // tm, N_EXPERTS),
        in_specs=[
            pl.BlockSpec((tm, D), lambda i, e: (i, 0)),
            pl.BlockSpec((1, D), lambda i, e: (0, 0)),
            pl.BlockSpec((D, ROUTER_LANES), lambda i, e: (0, 0)),
            pl.BlockSpec((D, ROUTER_LANES), lambda i, e: (0, 0)),
            pl.BlockSpec((1, D, 2 * D_EXPERT), lambda i, e: (e, 0, 0)),
            pl.BlockSpec((1, D_EXPERT, D), lambda i, e: (e, 0, 0)),
        ],
        out_specs=pl.BlockSpec((tm, D), lambda i, e: (i, 0)),
        out_shape=jax.ShapeDtypeStruct((N, D), F32),
        scratch_shapes=[pltpu.VMEM((tm, D), BF16), pltpu.VMEM((tm, ROUTER_LANES), F32)],
        compiler_params=_vmem_limit(48),
        name="moe_dense",
    )(h, g.reshape(1, D), wrh, wrl, w_gu.astype(BF16), w_d.astype(BF16))


def _ple_update(h_ref, p_ref, g_ref, wg_ref, wp_ref):
    h = h_ref[...]
    xn = (h * _rms_scale(h) * g_ref[...]).astype(BF16)
    gate = jax.nn.sigmoid(_dot(xn, wg_ref[...]))
    proj = _dot(p_ref[...].astype(BF16), wp_ref[...])
    return h + proj * gate


def _ple_qkv_kernel(h_ref, p_ref, g_ref, wg_ref, wp_ref, gkv_ref, wkv_ref, gq_ref, wq_ref,
                    ho_ref, q_ref, kv_ref):
    hn = _ple_update(h_ref, p_ref, g_ref, wg_ref, wp_ref)
    ho_ref[...] = hn
    base = hn * _rms_scale(hn)
    kv_ref[...] = _dot((base * gkv_ref[...]).astype(BF16), wkv_ref[...]).astype(BF16)
    q = _dot((base * gq_ref[...]).astype(BF16), wq_ref[...])
    q_ref[...] = (q * (1.0 / math.sqrt(HEAD_DIM))).astype(BF16)


def _ple_final_kernel(h_ref, p_ref, g_ref, wg_ref, wp_ref, gf_ref, o_ref):
    hn = _ple_update(h_ref, p_ref, g_ref, wg_ref, wp_ref)
    o_ref[...] = hn * _rms_scale(hn) * gf_ref[...]


def _ple_qkv(h, p, g_ple, w_gate, w_proj, g_kv, w_kv, g_q, w_q, *, tm=512):
    N, D = h.shape
    tok = lambda w: pl.BlockSpec((tm, w), lambda i: (i, 0))
    const = lambda a, b: pl.BlockSpec((a, b), lambda i: (0, 0))
    row = lambda v: v.reshape(1, D)
    return pl.pallas_call(
        _ple_qkv_kernel,
        grid=(N // tm,),
        in_specs=[tok(D), tok(PLE_DIM), const(1, D), const(D, D), const(PLE_DIM, D),
                  const(1, D), const(D, 2 * D), const(1, D), const(D, D)],
        out_specs=[tok(D), tok(D), tok(2 * D)],
        out_shape=[jax.ShapeDtypeStruct((N, D), F32), jax.ShapeDtypeStruct((N, D), BF16),
                   jax.ShapeDtypeStruct((N, 2 * D), BF16)],
        compiler_params=_vmem_limit(48),
        name="ple_qkv",
    )(h, p, row(g_ple), w_gate.astype(BF16), w_proj.astype(BF16),
      row(g_kv), w_kv.astype(BF16), row(g_q), w_q.astype(BF16))


def _ple_final(h, p, g_ple, w_gate, w_proj, g_final, *, tm=512):
    N, D = h.shape
    tok = lambda w: pl.BlockSpec((tm, w), lambda i: (i, 0))
    const = lambda a, b: pl.BlockSpec((a, b), lambda i: (0, 0))
    row = lambda v: v.reshape(1, D)
    return pl.pallas_call(
        _ple_final_kernel,
        grid=(N // tm,),
        in_specs=[tok(D), tok(PLE_DIM), const(1, D), const(D, D), const(PLE_DIM, D), const(1, D)],
        out_specs=tok(D),
        out_shape=jax.ShapeDtypeStruct((N, D), F32),
        compiler_params=_vmem_limit(40),
        name="ple_final",
    )(h, p, row(g_ple), w_gate.astype(BF16), w_proj.astype(BF16), row(g_final))


def _attn_kernel(q_ref, k_ref, v_ref, o_ref, *, tq, chunk):
    c = pl.program_id(2)
    lane = lax.broadcasted_iota(jnp.int32, (1, LANES), 1)
    even = lane < HEAD_DIM
    rows = lax.broadcasted_iota(jnp.int32, (2 * tq, tq), 0)
    cols = lax.broadcasted_iota(jnp.int32, (2 * tq, tq), 1)
    causal = cols < jnp.where(rows >= tq, rows - tq, rows)
    kj = lax.broadcasted_iota(jnp.int32, (tq, tq), 0)
    ks = lax.broadcasted_iota(jnp.int32, (tq, tq), 1)
    tri = (kj > ks).astype(BF16)
    ones = jnp.ones((tq, tq), BF16)

    def block(q2, j, carry, o, diag):
        kb = k_ref[0, pl.ds(pl.multiple_of(j * tq, tq), tq), :]
        vb = v_ref[0, pl.ds(pl.multiple_of(j * tq, tq), tq), :]
        z = lax.dot_general(q2, kb, (((1,), (1,)), ((), ())), preferred_element_type=F32)
        sp = jnp.maximum(z, 0.0) + jnp.log(1.0 + jnp.exp(-jnp.abs(z)))
        logsig = z - sp
        if diag:
            sp = jnp.where(causal, sp, 0.0)
        sph, spl = _split_bf16(sp)
        suffix = _dot(sph, tri) + _dot(spl, tri) + carry
        a = jnp.exp(logsig - suffix)
        if diag:
            a = jnp.where(causal, a, 0.0)
        o = o + _dot(a.astype(BF16), vb)
        carry = carry + _dot(sph, ones) + _dot(spl, ones)
        return carry, o

    def qblock(ib, _):
        r0 = pl.multiple_of(ib * tq, tq)
        qb = q_ref[0, pl.ds(r0, tq), :]
        zero = jnp.zeros_like(qb)
        q2 = jnp.concatenate([jnp.where(even, qb, zero), jnp.where(even, zero, qb)], axis=0)
        jd = c * (chunk // tq) + ib
        carry0, o0 = block(q2, jd, jnp.zeros((2 * tq, tq), F32),
                           jnp.zeros((2 * tq, LANES), F32), True)

        def cond(st):
            j, carry, _ = st
            return jnp.logical_and(j >= 0, jnp.min(carry) < SUFFIX_CUTOFF)

        def body(st):
            j, carry, o = st
            carry, o = block(q2, j, carry, o, False)
            return j - 1, carry, o

        _, _, o = lax.while_loop(cond, body, (jd - 1, carry0, o0))
        o_ref[0, pl.ds(r0, tq), :] = jnp.where(even, o[:tq], o[tq:]).astype(o_ref.dtype)
        return 0

    lax.fori_loop(0, chunk // tq, qblock, 0)


def _attention(q, kv, *, tq=128, chunk=1024):
    B, S, D = q.shape
    n_pairs = D // LANES
    return pl.pallas_call(
        functools.partial(_attn_kernel, tq=tq, chunk=chunk),
        grid=(B, n_pairs, S // chunk),
        in_specs=[
            pl.BlockSpec((1, chunk, LANES), lambda b, hp, c: (b, c, hp)),
            pl.BlockSpec((1, S, LANES), lambda b, hp, c: (b, 0, hp)),
            pl.BlockSpec((1, S, LANES), lambda b, hp, c: (b, 0, n_pairs + hp)),
        ],
        out_specs=pl.BlockSpec((1, chunk, LANES), lambda b, hp, c: (b, c, hp)),
        out_shape=jax.ShapeDtypeStruct((B, S, D), BF16),
        compiler_params=_vmem_limit(48),
        name="stickbreak_attn",
    )(q, kv, kv)


def _oproj_kernel(h_ref, o_ref, w_ref, out_ref):
    out_ref[...] = h_ref[...] + _dot(o_ref[...], w_ref[...])


def _oproj(h, o, w_o, *, tm=1024):
    N, D = h.shape
    tok = pl.BlockSpec((tm, D), lambda i: (i, 0))
    return pl.pallas_call(
        _oproj_kernel,
        grid=(N // tm,),
        in_specs=[tok, tok, pl.BlockSpec((D, D), lambda i: (0, 0))],
        out_specs=tok,
        out_shape=jax.ShapeDtypeStruct((N, D), F32),
        compiler_params=_vmem_limit(40),
        name="attn_oproj",
    )(h, o, w_o.astype(BF16))


def kernel(x, p, g_mix_a, w_pool, b_pool, ls_pool, g_kv, w_kv, g_mix_b, w_q, w_o, g_moe, w_router_group, w_router_expert, w_gate_up, w_down, g_ple, w_ple_gate, w_ple_proj, g_final):
    B, S, D = x.shape
    N = B * S
    assert p.shape[0] == 2 and g_mix_a.shape[0] == 1 and g_mix_b.shape[0] == 1
    p = p.reshape(p.shape[0], N, PLE_DIM)

    h = _pool_mixer(x, g_mix_a[0], w_pool[0], b_pool[0].reshape(-1), ls_pool[0]).reshape(N, D)
    h = _moe(h, g_moe[0], w_router_group[0], w_router_expert[0], w_gate_up[0], w_down[0])
    h, q, kv = _ple_qkv(h, p[0], g_ple[0], w_ple_gate[0], w_ple_proj[0],
                        g_kv, w_kv, g_mix_b[0], w_q[0])

    o = _attention(q.reshape(B, S, D), kv.reshape(B, S, 2 * D)).reshape(N, D)
    h = _oproj(h, o, w_o[0])
    h = _moe(h, g_moe[1], w_router_group[1], w_router_expert[1], w_gate_up[1], w_down[1])
    out = _ple_final(h, p[1], g_ple[1], w_ple_gate[1], w_ple_proj[1], g_final)
    return out.reshape(B, S, D)
```

```python
import functools
import math

import jax
import jax.numpy as jnp
from jax import lax
from jax.experimental import pallas as pl
from jax.experimental.pallas import tpu as pltpu

D_MODEL = 1024
POOL_WINDOWS = (2, 4, 8, 16)
POOL_GROUP_DIM = D_MODEL // len(POOL_WINDOWS)
POOL_HALO = 16
HEAD_DIM = 64
N_GROUPS = 4
EXPERTS_PER_GROUP = 8
N_EXPERTS = N_GROUPS * EXPERTS_PER_GROUP
D_EXPERT = D_MODEL // 4
PLE_DIM = 256
RMS_EPS = 1e-6
LANES = 128
ROUTER_LANES = LANES
SUFFIX_CUTOFF = 106.0

F32 = jnp.float32
BF16 = jnp.bfloat16


def _vmem_limit(mib):
    return pltpu.CompilerParams(vmem_limit_bytes=mib * 1024 * 1024)


def _rms_scale(x):
    return lax.rsqrt(jnp.mean(x * x, axis=-1, keepdims=True) + RMS_EPS)


def _split_bf16(x):
    hi = x.astype(BF16)
    lo = (x - hi.astype(F32)).astype(BF16)
    return hi, lo


def _dot(a, b):
    return jnp.dot(a, b, preferred_element_type=F32)


def _pool_kernel(x_ref, xprev_ref, g_ref, w_ref, b_ref, ls_ref, o_ref, *, ts):
    i = pl.program_id(1)
    xc = x_ref[0]
    xp = xprev_ref[0]
    g = g_ref[...]
    xn_c = xc * _rms_scale(xc) * g
    xn_p = xp * _rms_scale(xp) * g * (i > 0).astype(F32)
    full = jnp.concatenate([xn_p, xn_c], axis=0)
    t_glob = i * ts + lax.broadcasted_iota(jnp.int32, (ts, 1), 0)
    ys = []
    for gi, w in enumerate(POOL_WINDOWS):
        f = full[:, gi * POOL_GROUP_DIM:(gi + 1) * POOL_GROUP_DIM]
        s = f
        k = 1
        while k < w:
            s = s + pltpu.roll(s, k, axis=0)
            k *= 2
        cnt = jnp.minimum(t_glob + 1, w).astype(F32)
        pooled = s[POOL_HALO:] / cnt - f[POOL_HALO:]
        ys.append(_dot(pooled.astype(BF16), w_ref[gi]))
    y = jnp.concatenate(ys, axis=1) + b_ref[...]
    o_ref[0] = xc + y * ls_ref[...]


def _pool_mixer(x, g, w_pool, b_pool, ls_pool, *, ts=512):
    B, S, D = x.shape
    row = lambda v: v.reshape(1, D)
    const = lambda shape: pl.BlockSpec(shape, lambda b, i: (0,) * len(shape))
    return pl.pallas_call(
        functools.partial(_pool_kernel, ts=ts),
        grid=(B, S // ts),
        in_specs=[
            pl.BlockSpec((1, ts, D), lambda b, i: (b, i, 0)),
            pl.BlockSpec((1, POOL_HALO, D),
                         lambda b, i: (b, jnp.maximum(i * (ts // POOL_HALO) - 1, 0), 0)),
            const((1, D)),
            const(w_pool.shape),
            const((1, D)),
            const((1, D)),
        ],
        out_specs=pl.BlockSpec((1, ts, D), lambda b, i: (b, i, 0)),
        out_shape=jax.ShapeDtypeStruct((B, S, D), F32),
        compiler_params=_vmem_limit(40),
        name="pool_mixer",
    )(x, x, row(g), w_pool.astype(BF16), row(b_pool), row(ls_pool))


def _route(logits):
    t = logits.shape[0]
    lane = lax.broadcasted_iota(jnp.int32, (t, ROUTER_LANES), 1).astype(F32)
    neg = -jnp.inf
    big = float(ROUTER_LANES)
    gl = jnp.where(lane < N_GROUPS, logits, neg)
    gmax = jnp.max(gl, axis=1, keepdims=True)
    g_top_p = 1.0 / jnp.sum(jnp.exp(gl - gmax), axis=1, keepdims=True)
    g_idx = jnp.min(jnp.where(gl == gmax, lane, big), axis=1, keepdims=True)
    lo = N_GROUPS + EXPERTS_PER_GROUP * g_idx
    el = jnp.where((lane >= lo) & (lane < lo + EXPERTS_PER_GROUP), logits, neg)
    v1 = jnp.max(el, axis=1, keepdims=True)
    i1 = jnp.min(jnp.where(el == v1, lane, big), axis=1, keepdims=True)
    el2 = jnp.where(lane == i1, neg, el)
    v2 = jnp.max(el2, axis=1, keepdims=True)
    i2 = jnp.min(jnp.where(el2 == v2, lane, big), axis=1, keepdims=True)
    ex = jnp.exp(v2 - v1)
    w1 = g_top_p / (1.0 + ex)
    w2 = g_top_p * ex / (1.0 + ex)
    return jnp.where(lane == i1, w1, 0.0) + jnp.where(lane == i2, w2, 0.0)


def _moe_kernel(h_ref, g_ref, wrh_ref, wrl_ref, wgu_ref, wd_ref, o_ref, xn_ref, comb_ref):
    e = pl.program_id(1)

    @pl.when(e == 0)
    def _():
        x = h_ref[...]
        xn = x * _rms_scale(x) * g_ref[...]
        xh, xl = _split_bf16(xn)
        wrh = wrh_ref[...]
        logits = _dot(xh, wrh) + _dot(xl, wrh) + _dot(xh, wrl_ref[...])
        comb_ref[...] = _route(logits)
        xn_ref[...] = xh
        o_ref[...] = x

    gu = _dot(xn_ref[...], wgu_ref[0])
    gate = gu[:, :D_EXPERT]
    hmid = gate * jax.nn.sigmoid(gate) * gu[:, D_EXPERT:]
    y = _dot(hmid.astype(BF16), wd_ref[0])
    lane = lax.broadcasted_iota(jnp.int32, (1, ROUTER_LANES), 1)
    c = jnp.sum(jnp.where(lane == e + N_GROUPS, comb_ref[...], 0.0), axis=1, keepdims=True)
    o_ref[...] += c * y


def _moe(h, g, w_rg, w_re, w_gu, w_d, *, tm=1024):
    N, D = h.shape
    w_r = jnp.concatenate([w_rg, w_re], axis=1)
    w_r = jnp.pad(w_r, ((0, 0), (0, ROUTER_LANES - w_r.shape[1])))
    wrh, wrl = _split_bf16(w_r)
    return pl.pallas_call(
        _moe_kernel,
        grid=(N // tm, N_EXPERTS),
        in_specs=[
            pl.BlockSpec((tm, D), lambda i, e: (i, 0)),
            pl.BlockSpec((1, D), lambda i, e: (0, 0)),
            pl.BlockSpec((D, ROUTER_LANES), lambda i, e: (0, 0)),
            pl.BlockSpec((D, ROUTER_LANES), lambda i, e: (0, 0)),
            pl.BlockSpec((1, D, 2 * D_EXPERT), lambda i, e: (e, 0, 0)),
            pl.BlockSpec((1, D_EXPERT, D), lambda i, e: (e, 0, 0)),
        ],
        out_specs=pl.BlockSpec((tm, D), lambda i, e: (i, 0)),
        out_shape=jax.ShapeDtypeStruct((N, D), F32),
        scratch_shapes=[pltpu.VMEM((tm, D), BF16), pltpu.VMEM((tm, ROUTER_LANES), F32)],
        compiler_params=_vmem_limit(48),
        name="moe_dense",
    )(h, g.reshape(1, D), wrh, wrl, w_gu.astype(BF16), w_d.astype(BF16))


def _ple_update(h_ref, p_ref, g_ref, wg_ref, wp_ref):
    h = h_ref[...]
    xn = (h * _rms_scale(h) * g_ref[...]).astype(BF16)
    gate = jax.nn.sigmoid(_dot(xn, wg_ref[...]))
    proj = _dot(p_ref[...].astype(BF16), wp_ref[...])
    return h + proj * gate


def _ple_qkv_kernel(h_ref, p_ref, g_ref, wg_ref, wp_ref, gkv_ref, wkv_ref, gq_ref, wq_ref,
                    ho_ref, q_ref, kv_ref):
    hn = _ple_update(h_ref, p_ref, g_ref, wg_ref, wp_ref)
    ho_ref[...] = hn
    base = hn * _rms_scale(hn)
    kv_ref[...] = _dot((base * gkv_ref[...]).astype(BF16), wkv_ref[...]).astype(BF16)
    q = _dot((base * gq_ref[...]).astype(BF16), wq_ref[...])
    q_ref[...] = (q * (1.0 / math.sqrt(HEAD_DIM))).astype(BF16)


def _ple_final_kernel(h_ref, p_ref, g_ref, wg_ref, wp_ref, gf_ref, o_ref):
    hn = _ple_update(h_ref, p_ref, g_ref, wg_ref, wp_ref)
    o_ref[...] = hn * _rms_scale(hn) * gf_ref[...]


def _ple_qkv(h, p, g_ple, w_gate, w_proj, g_kv, w_kv, g_q, w_q, *, tm=512):
    N, D = h.shape
    tok = lambda w: pl.BlockSpec((tm, w), lambda i: (i, 0))
    const = lambda a, b: pl.BlockSpec((a, b), lambda i: (0, 0))
    row = lambda v: v.reshape(1, D)
    return pl.pallas_call(
        _ple_qkv_kernel,
        grid=(N // tm,),
        in_specs=[tok(D), tok(PLE_DIM), const(1, D), const(D, D), const(PLE_DIM, D),
                  const(1, D), const(D, 2 * D), const(1, D), const(D, D)],
        out_specs=[tok(D), tok(D), tok(2 * D)],
        out_shape=[jax.ShapeDtypeStruct((N, D), F32), jax.ShapeDtypeStruct((N, D), BF16),
                   jax.ShapeDtypeStruct((N, 2 * D), BF16)],
        compiler_params=_vmem_limit(48),
        name="ple_qkv",
    )(h, p, row(g_ple), w_gate.astype(BF16), w_proj.astype(BF16),
      row(g_kv), w_kv.astype(BF16), row(g_q), w_q.astype(BF16))


def _ple_final(h, p, g_ple, w_gate, w_proj, g_final, *, tm=512):
    N, D = h.shape
    tok = lambda w: pl.BlockSpec((tm, w), lambda i: (i, 0))
    const = lambda a, b: pl.BlockSpec((a, b), lambda i: (0, 0))
    row = lambda v: v.reshape(1, D)
    return pl.pallas_call(
        _ple_final_kernel,
        grid=(N // tm,),
        in_specs=[tok(D), tok(PLE_DIM), const(1, D), const(D, D), const(PLE_DIM, D), const(1, D)],
        out_specs=tok(D),
        out_shape=jax.ShapeDtypeStruct((N, D), F32),
        compiler_params=_vmem_limit(40),
        name="ple_final",
    )(h, p, row(g_ple), w_gate.astype(BF16), w_proj.astype(BF16), row(g_final))


def _attn_kernel(q_ref, k_ref, v_ref, o_ref, *, tq, win, chunk, unroll):
    c = pl.program_id(2)
    nrow = 2 * tq
    lane = lax.broadcasted_iota(jnp.int32, (1, LANES), 1)
    even = lane < HEAD_DIM
    rows = lax.broadcasted_iota(jnp.int32, (nrow, win), 0)
    cols = lax.broadcasted_iota(jnp.int32, (nrow, win), 1)
    diag = cols - jnp.where(rows >= tq, rows - tq, rows)
    kj = lax.broadcasted_iota(jnp.int32, (win, win), 0)
    ks = lax.broadcasted_iota(jnp.int32, (win, win), 1)
    tri = (kj > ks).astype(BF16)

    def window(q2, ws, valid, carry):
        ws = pl.multiple_of(ws, tq)
        kb = k_ref[0, pl.ds(ws, win), :]
        vb = v_ref[0, pl.ds(ws, win), :]
        z = lax.dot_general(q2, kb, (((1,), (1,)), ((), ())), preferred_element_type=F32)
        sp = jnp.maximum(z, 0.0) + jnp.log(1.0 + jnp.exp(-jnp.abs(z)))
        logsig = z - sp
        sp = jnp.where(valid, sp, 0.0)
        suffix = _dot(sp.astype(BF16), tri)
        if carry is not None:
            suffix = suffix + carry
        a = jnp.where(valid, jnp.exp(logsig - suffix), 0.0)
        return _dot(a.astype(BF16), vb), suffix[:, :1] + sp[:, :1]

    def group(ig, _):
        q2s, ws0, outs, totals = [], [], [], []
        for u in range(unroll):
            r0 = pl.multiple_of((ig * unroll + u) * tq, tq)
            q0 = c * chunk + r0
            qb = q_ref[0, pl.ds(r0, tq), :]
            zero = jnp.zeros_like(qb)
            q2 = jnp.concatenate([jnp.where(even, qb, zero), jnp.where(even, zero, qb)], axis=0)
            ws = jnp.maximum(q0 + tq - win, 0)
            o, total = window(q2, ws, diag < q0 - ws, None)
            q2s.append(q2)
            ws0.append(ws)
            outs.append(o)
            totals.append(total)

        def least_mass(ts):
            return jnp.min(functools.reduce(jnp.minimum, ts))

        def cond(st):
            w, low, _, _ = st
            return jnp.logical_and(ws0[-1] - (w - 1) * win > 0, low < SUFFIX_CUTOFF)

        def body(st):
            w, _, outs, totals = st
            new_o, new_t = [], []
            for u in range(unroll):
                upper = jnp.maximum(ws0[u] - (w - 1) * win, 0)
                ws = jnp.maximum(upper - win, 0)
                o, total = window(q2s[u], ws, cols < upper - ws, totals[u])
                new_o.append(outs[u] + o)
                new_t.append(total)
            return w + 1, least_mass(new_t), tuple(new_o), tuple(new_t)

        _, _, outs, _ = lax.while_loop(
            cond, body, (jnp.int32(1), least_mass(totals), tuple(outs), tuple(totals)))
        for u in range(unroll):
            r0 = pl.multiple_of((ig * unroll + u) * tq, tq)
            o = outs[u]
            o_ref[0, pl.ds(r0, tq), :] = jnp.where(even, o[:tq], o[tq:]).astype(o_ref.dtype)
        return 0

    lax.fori_loop(0, chunk // (tq * unroll), group, 0)


def _attention(q, kv, *, tq=64, win=256, chunk=1024, unroll=4):
    B, S, D = q.shape
    n_pairs = D // LANES
    return pl.pallas_call(
        functools.partial(_attn_kernel, tq=tq, win=win, chunk=chunk, unroll=unroll),
        grid=(B, n_pairs, S // chunk),
        in_specs=[
            pl.BlockSpec((1, chunk, LANES), lambda b, hp, c: (b, c, hp)),
            pl.BlockSpec((1, S, LANES), lambda b, hp, c: (b, 0, hp)),
            pl.BlockSpec((1, S, LANES), lambda b, hp, c: (b, 0, n_pairs + hp)),
        ],
        out_specs=pl.BlockSpec((1, chunk, LANES), lambda b, hp, c: (b, c, hp)),
        out_shape=jax.ShapeDtypeStruct((B, S, D), BF16),
        compiler_params=_vmem_limit(48),
        name="stickbreak_attn",
    )(q, kv, kv)


def _oproj_kernel(h_ref, o_ref, w_ref, out_ref):
    out_ref[...] = h_ref[...] + _dot(o_ref[...], w_ref[...])


def _oproj(h, o, w_o, *, tm=1024):
    N, D = h.shape
    tok = pl.BlockSpec((tm, D), lambda i: (i, 0))
    return pl.pallas_call(
        _oproj_kernel,
        grid=(N // tm,),
        in_specs=[tok, tok, pl.BlockSpec((D, D), lambda i: (0, 0))],
        out_specs=tok,
        out_shape=jax.ShapeDtypeStruct((N, D), F32),
        compiler_params=_vmem_limit(40),
        name="attn_oproj",
    )(h, o, w_o.astype(BF16))


def kernel(x, p, g_mix_a, w_pool, b_pool, ls_pool, g_kv, w_kv, g_mix_b, w_q, w_o, g_moe, w_router_group, w_router_expert, w_gate_up, w_down, g_ple, w_ple_gate, w_ple_proj, g_final):
    B, S, D = x.shape
    N = B * S
    assert p.shape[0] == 2 and g_mix_a.shape[0] == 1 and g_mix_b.shape[0] == 1
    p = p.reshape(p.shape[0], N, PLE_DIM)

    h = _pool_mixer(x, g_mix_a[0], w_pool[0], b_pool[0].reshape(-1), ls_pool[0]).reshape(N, D)
    h = _moe(h, g_moe[0], w_router_group[0], w_router_expert[0], w_gate_up[0], w_down[0])
    h, q, kv = _ple_qkv(h, p[0], g_ple[0], w_ple_gate[0], w_ple_proj[0],
                        g_kv, w_kv, g_mix_b[0], w_q[0])

    o = _attention(q.reshape(B, S, D), kv.reshape(B, S, 2 * D)).reshape(N, D)
    h = _oproj(h, o, w_o[0])
    h = _moe(h, g_moe[1], w_router_group[1], w_router_expert[1], w_gate_up[1], w_down[1])
    out = _ple_final(h, p[1], g_ple[1], w_ple_gate[1], w_ple_proj[1], g_final)
    return out.reshape(B, S, D)
```

```python
import functools
import math

import jax
import jax.numpy as jnp
from jax import lax
from jax.experimental import pallas as pl
from jax.experimental.pallas import tpu as pltpu

D_MODEL = 1024
POOL_WINDOWS = (2, 4, 8, 16)
POOL_GROUP_DIM = D_MODEL // len(POOL_WINDOWS)
POOL_HALO = 16
HEAD_DIM = 64
N_GROUPS = 4
EXPERTS_PER_GROUP = 8
N_EXPERTS = N_GROUPS * EXPERTS_PER_GROUP
D_EXPERT = D_MODEL // 4
PLE_DIM = 256
RMS_EPS = 1e-6
LANES = 128
ROUTER_LANES = LANES
SUBROWS = D_MODEL // LANES
DMA_UNROLL = 8
SUFFIX_CUTOFF = 106.0

F32 = jnp.float32
BF16 = jnp.bfloat16


def _vmem_limit(mib):
    return pltpu.CompilerParams(vmem_limit_bytes=mib * 1024 * 1024)


def _rms_scale(x):
    return lax.rsqrt(jnp.mean(x * x, axis=-1, keepdims=True) + RMS_EPS)


def _split_bf16(x):
    hi = x.astype(BF16)
    lo = (x - hi.astype(F32)).astype(BF16)
    return hi, lo


def _dot(a, b):
    return jnp.dot(a, b, preferred_element_type=F32)


def _pool_kernel(x_ref, xprev_ref, g_ref, w_ref, b_ref, ls_ref, o_ref, *, ts):
    i = pl.program_id(1)
    xc = x_ref[0]
    xp = xprev_ref[0]
    g = g_ref[...]
    xn_c = xc * _rms_scale(xc) * g
    xn_p = xp * _rms_scale(xp) * g * (i > 0).astype(F32)
    full = jnp.concatenate([xn_p, xn_c], axis=0)
    t_glob = i * ts + lax.broadcasted_iota(jnp.int32, (ts, 1), 0)
    ys = []
    for gi, w in enumerate(POOL_WINDOWS):
        f = full[:, gi * POOL_GROUP_DIM:(gi + 1) * POOL_GROUP_DIM]
        s = f
        k = 1
        while k < w:
            s = s + pltpu.roll(s, k, axis=0)
            k *= 2
        cnt = jnp.minimum(t_glob + 1, w).astype(F32)
        pooled = s[POOL_HALO:] / cnt - f[POOL_HALO:]
        ys.append(_dot(pooled.astype(BF16), w_ref[gi]))
    y = jnp.concatenate(ys, axis=1) + b_ref[...]
    o_ref[0] = xc + y * ls_ref[...]


def _pool_mixer(x, g, w_pool, b_pool, ls_pool, *, ts=512):
    B, S, D = x.shape
    row = lambda v: v.reshape(1, D)
    const = lambda shape: pl.BlockSpec(shape, lambda b, i: (0,) * len(shape))
    return pl.pallas_call(
        functools.partial(_pool_kernel, ts=ts),
        grid=(B, S // ts),
        in_specs=[
            pl.BlockSpec((1, ts, D), lambda b, i: (b, i, 0)),
            pl.BlockSpec((1, POOL_HALO, D),
                         lambda b, i: (b, jnp.maximum(i * (ts // POOL_HALO) - 1, 0), 0)),
            const((1, D)),
            const(w_pool.shape),
            const((1, D)),
            const((1, D)),
        ],
        out_specs=pl.BlockSpec((1, ts, D), lambda b, i: (b, i, 0)),
        out_shape=jax.ShapeDtypeStruct((B, S, D), F32),
        compiler_params=_vmem_limit(40),
        name="pool_mixer",
    )(x, x, row(g), w_pool.astype(BF16), row(b_pool), row(ls_pool))


def _to_rows(ref, x, n):
    for c in range(SUBROWS):
        ref[pl.ds(c, n, stride=SUBROWS), :] = x[:, c * LANES:(c + 1) * LANES]


def _from_rows(ref, n, start, stride):
    return jnp.concatenate(
        [ref[pl.ds(start + c, n, stride=stride), :] for c in range(SUBROWS)], axis=1)


def _top2(logits):
    t = logits.shape[0]
    lane = lax.broadcasted_iota(jnp.int32, (t, ROUTER_LANES), 1).astype(F32)
    neg = -jnp.inf
    big = float(ROUTER_LANES)
    gl = jnp.where(lane < N_GROUPS, logits, neg)
    gmax = jnp.max(gl, axis=1, keepdims=True)
    g_top_p = 1.0 / jnp.sum(jnp.exp(gl - gmax), axis=1, keepdims=True)
    g_idx = jnp.min(jnp.where(gl == gmax, lane, big), axis=1, keepdims=True)
    lo = N_GROUPS + EXPERTS_PER_GROUP * g_idx
    el = jnp.where((lane >= lo) & (lane < lo + EXPERTS_PER_GROUP), logits, neg)
    v1 = jnp.max(el, axis=1, keepdims=True)
    i1 = jnp.min(jnp.where(el == v1, lane, big), axis=1, keepdims=True)
    el2 = jnp.where(lane == i1, neg, el)
    v2 = jnp.max(el2, axis=1, keepdims=True)
    i2 = jnp.min(jnp.where(el2 == v2, lane, big), axis=1, keepdims=True)
    ex = jnp.exp(v2 - v1)
    return i1 - N_GROUPS, i2 - N_GROUPS, g_top_p / (1.0 + ex), g_top_p * ex / (1.0 + ex)


def _route_kernel(h_ref, g_ref, wrh_ref, wrl_ref, route_ref, cnt_ref, run_ref, *, tm):
    @pl.when(pl.program_id(0) == 0)
    def _():
        run_ref[...] = jnp.zeros_like(run_ref)

    x = h_ref[...]
    xh, xl = _split_bf16(x * _rms_scale(x) * g_ref[...])
    wrh = wrh_ref[...]
    e1, e2, w1, w2 = _top2(_dot(xh, wrh) + _dot(xl, wrh) + _dot(xh, wrl_ref[...]))
    lane = lax.broadcasted_iota(jnp.int32, (tm, ROUTER_LANES), 1).astype(F32)
    oh1 = lane == e1
    oh2 = lane == e2
    both = jnp.where(oh1 | oh2, 1.0, 0.0)
    earlier = (lax.broadcasted_iota(jnp.int32, (tm, tm), 0)
               > lax.broadcasted_iota(jnp.int32, (tm, tm), 1)).astype(BF16)
    before = _dot(earlier, both.astype(BF16)) + run_ref[...]
    r1 = jnp.sum(jnp.where(oh1, before, 0.0), axis=1, keepdims=True)
    r2 = jnp.sum(jnp.where(oh2, before, 0.0), axis=1, keepdims=True)
    run_ref[...] += jnp.sum(both, axis=0, keepdims=True)
    cnt_ref[...] = run_ref[...]
    fields = (e1, e2, w1, w2, r1, r2)
    out = jnp.zeros((tm, ROUTER_LANES), F32)
    for k, f in enumerate(fields):
        out = jnp.where(lane == k, f, out)
    route_ref[...] = out


def _route(h, g, w_rg, w_re, *, tm=512):
    N, D = h.shape
    w_r = jnp.concatenate([w_rg, w_re], axis=1)
    w_r = jnp.pad(w_r, ((0, 0), (0, ROUTER_LANES - w_r.shape[1])))
    wrh, wrl = _split_bf16(w_r)
    const = lambda a, b: pl.BlockSpec((a, b), lambda i: (0, 0))
    return pl.pallas_call(
        functools.partial(_route_kernel, tm=tm),
        grid=(N // tm,),
        in_specs=[pl.BlockSpec((tm, D), lambda i: (i, 0)), const(1, D),
                  const(D, ROUTER_LANES), const(D, ROUTER_LANES)],
        out_specs=[pl.BlockSpec((tm, ROUTER_LANES), lambda i: (i, 0)), const(1, ROUTER_LANES)],
        out_shape=[jax.ShapeDtypeStruct((N, ROUTER_LANES), F32),
                   jax.ShapeDtypeStruct((1, ROUTER_LANES), F32)],
        scratch_shapes=[pltpu.VMEM((1, ROUTER_LANES), F32)],
        compiler_params=_vmem_limit(32),
        name="moe_route",
    )(h, g.reshape(1, D), wrh, wrl)


def _dispatch_plan(route, counts, tm, n_tiles):
    e = route[:, 0:2].astype(jnp.int32)
    rank = route[:, 4:6].astype(jnp.int32)
    cnt = counts[0, :N_EXPERTS].astype(jnp.int32)
    tiles = (cnt + tm - 1) // tm
    tile_end = jnp.cumsum(tiles)
    pos = (jnp.take(tile_end - tiles, e) * tm + rank).reshape(-1)
    t = jnp.arange(n_tiles, dtype=jnp.int32)
    tile_expert = jnp.sum((t[:, None] >= tile_end[None, :]).astype(jnp.int32), axis=1)
    return pos, jnp.minimum(tile_expert, N_EXPERTS - 1), tile_end[-1:].astype(jnp.int32)


def _row_copy(src, src_row, dst, dst_row, sem):
    return pltpu.make_async_copy(
        src.at[pl.ds(pl.multiple_of(src_row * SUBROWS, SUBROWS), SUBROWS), :],
        dst.at[pl.ds(pl.multiple_of(dst_row * SUBROWS, SUBROWS), SUBROWS), :], sem)


def _dispatch_kernel(pos_ref, h_ref, g_ref, xs_in, xs_out, stage, sem, *, tm):
    del xs_in
    i = pl.program_id(0)
    n = pl.num_programs(0)
    slot = i % 2

    def drain(s):
        for _ in range(2):
            pltpu.make_async_copy(stage.at[s], xs_out.at[pl.ds(0, tm * SUBROWS), :],
                                  sem.at[s]).wait()

    @pl.when(i >= 2)
    def _():
        drain(slot)

    x = h_ref[...]
    buf = stage.at[slot]
    _to_rows(buf, x * _rms_scale(x) * g_ref[...], tm)

    def issue(jj, _):
        for u in range(DMA_UNROLL):
            j = jj * DMA_UNROLL + u
            for k in range(2):
                _row_copy(buf, j, xs_out, pos_ref[2 * (i * tm + j) + k], sem.at[slot]).start()
        return 0

    lax.fori_loop(0, tm // DMA_UNROLL, issue, 0)

    @pl.when(i == n - 1)
    def _():
        drain(slot)

        @pl.when(n > 1)
        def _():
            drain(1 - slot)


def _dispatch(h, g, pos, n_rows, *, tm=256):
    N, D = h.shape
    grid_spec = pltpu.PrefetchScalarGridSpec(
        num_scalar_prefetch=1,
        grid=(N // tm,),
        in_specs=[pl.BlockSpec((tm, D), lambda i, pos: (i, 0)),
                  pl.BlockSpec((1, D), lambda i, pos: (0, 0)),
                  pl.BlockSpec(memory_space=pl.ANY)],
        out_specs=pl.BlockSpec(memory_space=pl.ANY),
        scratch_shapes=[pltpu.VMEM((2, tm * SUBROWS, LANES), F32), pltpu.SemaphoreType.DMA((2,))],
    )
    return pl.pallas_call(
        functools.partial(_dispatch_kernel, tm=tm),
        grid_spec=grid_spec,
        out_shape=jax.ShapeDtypeStruct((n_rows * SUBROWS, LANES), F32),
        input_output_aliases={3: 0},
        compiler_params=_vmem_limit(32),
        name="moe_dispatch",
    )(pos, h, g.reshape(1, D), jnp.zeros((n_rows * SUBROWS, LANES), F32))


def _experts_kernel(te_ref, nu_ref, x_ref, wgu_ref, wd_ref, y_ref, *, tm):
    del te_ref

    @pl.when(pl.program_id(0) < nu_ref[0])
    def _():
        x = _from_rows(x_ref, tm, 0, SUBROWS).astype(BF16)
        gu = _dot(x, wgu_ref[0])
        gate = gu[:, :D_EXPERT]
        hmid = gate * jax.nn.sigmoid(gate) * gu[:, D_EXPERT:]
        _to_rows(y_ref, _dot(hmid.astype(BF16), wd_ref[0]), tm)


def _experts(xs, tile_expert, n_used, w_gu, w_d, *, tm):
    n_tiles = tile_expert.shape[0]
    D = D_MODEL
    last = lambda i, nu: jnp.minimum(i, nu[0] - 1)
    grid_spec = pltpu.PrefetchScalarGridSpec(
        num_scalar_prefetch=2,
        grid=(n_tiles,),
        in_specs=[pl.BlockSpec((tm * SUBROWS, LANES), lambda i, te, nu: (last(i, nu), 0)),
                  pl.BlockSpec((1, D, 2 * D_EXPERT), lambda i, te, nu: (te[last(i, nu)], 0, 0)),
                  pl.BlockSpec((1, D_EXPERT, D), lambda i, te, nu: (te[last(i, nu)], 0, 0))],
        out_specs=pl.BlockSpec((tm * SUBROWS, LANES), lambda i, te, nu: (last(i, nu), 0)),
    )
    return pl.pallas_call(
        functools.partial(_experts_kernel, tm=tm),
        grid_spec=grid_spec,
        out_shape=jax.ShapeDtypeStruct(xs.shape, F32),
        input_output_aliases={2: 0},
        compiler_params=_vmem_limit(32),
        name="moe_experts",
    )(tile_expert, n_used, xs, w_gu.astype(BF16), w_d.astype(BF16))


def _moe_rows(h, g, w_rg, w_re, w_gu, w_d, *, tm=256):
    N = h.shape[0]
    n_tiles = 2 * N // tm + N_EXPERTS
    route, counts = _route(h, g, w_rg, w_re)
    pos, tile_expert, n_used = _dispatch_plan(route, counts, tm, n_tiles)
    xs = _dispatch(h, g, pos, n_tiles * tm)
    return route, pos, _experts(xs, tile_expert, n_used, w_gu, w_d, tm=tm)


def _combine_ple(pos_ref, h_ref, route_ref, p_ref, g_ref, wg_ref, wp_ref, ys_hbm, buf, sem, *, tm):
    i = pl.program_id(0)
    n = pl.num_programs(0)
    slot = i % 2

    def fetch(step, s):
        def issue(jj, _):
            for u in range(DMA_UNROLL):
                j = jj * DMA_UNROLL + u
                _row_copy(ys_hbm, pos_ref[2 * step * tm + j], buf.at[s], j, sem.at[s]).start()
            return 0
        lax.fori_loop(0, 2 * tm // DMA_UNROLL, issue, 0)

    @pl.when(i == 0)
    def _():
        fetch(0, 0)

    @pl.when(i + 1 < n)
    def _():
        fetch(i + 1, 1 - slot)

    rows = buf.at[slot]
    pltpu.make_async_copy(ys_hbm.at[pl.ds(0, 2 * tm * SUBROWS), :], rows, sem.at[slot]).wait()
    r = route_ref[...]
    h = (h_ref[...] + r[:, 2:3] * _from_rows(rows, tm, 0, 2 * SUBROWS)
         + r[:, 3:4] * _from_rows(rows, tm, SUBROWS, 2 * SUBROWS))
    xn = (h * _rms_scale(h) * g_ref[...]).astype(BF16)
    gate = jax.nn.sigmoid(_dot(xn, wg_ref[...]))
    proj = _dot(p_ref[...].astype(BF16), wp_ref[...])
    return h + proj * gate


def _ple_qkv_kernel(pos_ref, h_ref, route_ref, p_ref, g_ref, wg_ref, wp_ref, gkv_ref, wkv_ref,
                    gq_ref, wq_ref, ys_hbm, ho_ref, q_ref, kv_ref, buf, sem, *, tm):
    hn = _combine_ple(pos_ref, h_ref, route_ref, p_ref, g_ref, wg_ref, wp_ref, ys_hbm, buf, sem,
                      tm=tm)
    ho_ref[...] = hn
    base = hn * _rms_scale(hn)
    kv_ref[...] = _dot((base * gkv_ref[...]).astype(BF16), wkv_ref[...]).astype(BF16)
    q = _dot((base * gq_ref[...]).astype(BF16), wq_ref[...])
    q_ref[...] = (q * (1.0 / math.sqrt(HEAD_DIM))).astype(BF16)


def _ple_final_kernel(pos_ref, h_ref, route_ref, p_ref, g_ref, wg_ref, wp_ref, gf_ref, ys_hbm,
                      o_ref, buf, sem, *, tm):
    hn = _combine_ple(pos_ref, h_ref, route_ref, p_ref, g_ref, wg_ref, wp_ref, ys_hbm, buf, sem,
                      tm=tm)
    o_ref[...] = hn * _rms_scale(hn) * gf_ref[...]


def _ple_call(body, name, pos, token_inputs, const_inputs, ys, out_widths, out_dtypes, *, tm=256):
    N = token_inputs[0].shape[0]
    tok = lambda w: pl.BlockSpec((tm, w), lambda i, pos: (i, 0))
    const = lambda a: pl.BlockSpec(a.shape, lambda i, pos: (0, 0))
    grid_spec = pltpu.PrefetchScalarGridSpec(
        num_scalar_prefetch=1,
        grid=(N // tm,),
        in_specs=([tok(a.shape[1]) for a in token_inputs] + [const(a) for a in const_inputs]
                  + [pl.BlockSpec(memory_space=pl.ANY)]),
        out_specs=[tok(w) for w in out_widths],
        scratch_shapes=[pltpu.VMEM((2, 2 * tm * SUBROWS, LANES), F32),
                        pltpu.SemaphoreType.DMA((2,))],
    )
    return pl.pallas_call(
        functools.partial(body, tm=tm),
        grid_spec=grid_spec,
        out_shape=[jax.ShapeDtypeStruct((N, w), dt) for w, dt in zip(out_widths, out_dtypes)],
        compiler_params=_vmem_limit(48),
        name=name,
    )(pos, *token_inputs, *const_inputs, ys)


def _ple_qkv(h, route, pos, ys, p, g_ple, w_gate, w_proj, g_kv, w_kv, g_q, w_q):
    D = h.shape[1]
    row = lambda v: v.reshape(1, D)
    consts = [row(g_ple), w_gate.astype(BF16), w_proj.astype(BF16),
              row(g_kv), w_kv.astype(BF16), row(g_q), w_q.astype(BF16)]
    return _ple_call(_ple_qkv_kernel, "ple_qkv", pos, [h, route, p], consts, ys,
                     (D, D, 2 * D), (F32, BF16, BF16))


def _ple_final(h, route, pos, ys, p, g_ple, w_gate, w_proj, g_final):
    D = h.shape[1]
    row = lambda v: v.reshape(1, D)
    consts = [row(g_ple), w_gate.astype(BF16), w_proj.astype(BF16), row(g_final)]
    return _ple_call(_ple_final_kernel, "ple_final", pos, [h, route, p], consts, ys,
                     (D,), (F32,))[0]


def _attn_kernel(q_ref, k_ref, v_ref, o_ref, *, tq, win, chunk, unroll):
    c = pl.program_id(2)
    nrow = 2 * tq
    lane = lax.broadcasted_iota(jnp.int32, (1, LANES), 1)
    even = lane < HEAD_DIM
    rows = lax.broadcasted_iota(jnp.int32, (nrow, win), 0)
    cols = lax.broadcasted_iota(jnp.int32, (nrow, win), 1)
    diag = cols - jnp.where(rows >= tq, rows - tq, rows)
    kj = lax.broadcasted_iota(jnp.int32, (win, win), 0)
    ks = lax.broadcasted_iota(jnp.int32, (win, win), 1)
    tri = (kj > ks).astype(BF16)

    def window(q2, ws, valid, carry):
        ws = pl.multiple_of(ws, tq)
        kb = k_ref[0, pl.ds(ws, win), :]
        vb = v_ref[0, pl.ds(ws, win), :]
        z = lax.dot_general(q2, kb, (((1,), (1,)), ((), ())), preferred_element_type=F32)
        sp = jnp.maximum(z, 0.0) + jnp.log(1.0 + jnp.exp(-jnp.abs(z)))
        logsig = z - sp
        sp = jnp.where(valid, sp, 0.0)
        suffix = _dot(sp.astype(BF16), tri)
        if carry is not None:
            suffix = suffix + carry
        a = jnp.where(valid, jnp.exp(logsig - suffix), 0.0)
        return _dot(a.astype(BF16), vb), suffix[:, :1] + sp[:, :1]

    def group(ig, _):
        q2s, ws0, outs, totals = [], [], [], []
        for u in range(unroll):
            r0 = pl.multiple_of((ig * unroll + u) * tq, tq)
            q0 = c * chunk + r0
            qb = q_ref[0, pl.ds(r0, tq), :]
            zero = jnp.zeros_like(qb)
            q2 = jnp.concatenate([jnp.where(even, qb, zero), jnp.where(even, zero, qb)], axis=0)
            ws = jnp.maximum(q0 + tq - win, 0)
            o, total = window(q2, ws, diag < q0 - ws, None)
            q2s.append(q2)
            ws0.append(ws)
            outs.append(o)
            totals.append(total)

        def least_mass(ts):
            return jnp.min(functools.reduce(jnp.minimum, ts))

        def cond(st):
            w, low, _, _ = st
            return jnp.logical_and(ws0[-1] - (w - 1) * win > 0, low < SUFFIX_CUTOFF)

        def body(st):
            w, _, outs, totals = st
            new_o, new_t = [], []
            for u in range(unroll):
                upper = jnp.maximum(ws0[u] - (w - 1) * win, 0)
                ws = jnp.maximum(upper - win, 0)
                o, total = window(q2s[u], ws, cols < upper - ws, totals[u])
                new_o.append(outs[u] + o)
                new_t.append(total)
            return w + 1, least_mass(new_t), tuple(new_o), tuple(new_t)

        _, _, outs, _ = lax.while_loop(
            cond, body, (jnp.int32(1), least_mass(totals), tuple(outs), tuple(totals)))
        for u in range(unroll):
            r0 = pl.multiple_of((ig * unroll + u) * tq, tq)
            o = outs[u]
            o_ref[0, pl.ds(r0, tq), :] = jnp.where(even, o[:tq], o[tq:]).astype(o_ref.dtype)
        return 0

    lax.fori_loop(0, chunk // (tq * unroll), group, 0)


def _attention(q, kv, *, tq=64, win=256, chunk=1024, unroll=4):
    B, S, D = q.shape
    n_pairs = D // LANES
    return pl.pallas_call(
        functools.partial(_attn_kernel, tq=tq, win=win, chunk=chunk, unroll=unroll),
        grid=(B, n_pairs, S // chunk),
        in_specs=[
            pl.BlockSpec((1, chunk, LANES), lambda b, hp, c: (b, c, hp)),
            pl.BlockSpec((1, S, LANES), lambda b, hp, c: (b, 0, hp)),
            pl.BlockSpec((1, S, LANES), lambda b, hp, c: (b, 0, n_pairs + hp)),
        ],
        out_specs=pl.BlockSpec((1, chunk, LANES), lambda b, hp, c: (b, c, hp)),
        out_shape=jax.ShapeDtypeStruct((B, S, D), BF16),
        compiler_params=_vmem_limit(48),
        name="stickbreak_attn",
    )(q, kv, kv)


def _oproj_kernel(h_ref, o_ref, w_ref, out_ref):
    out_ref[...] = h_ref[...] + _dot(o_ref[...], w_ref[...])


def _oproj(h, o, w_o, *, tm=1024):
    N, D = h.shape
    tok = pl.BlockSpec((tm, D), lambda i: (i, 0))
    return pl.pallas_call(
        _oproj_kernel,
        grid=(N // tm,),
        in_specs=[tok, tok, pl.BlockSpec((D, D), lambda i: (0, 0))],
        out_specs=tok,
        out_shape=jax.ShapeDtypeStruct((N, D), F32),
        compiler_params=_vmem_limit(40),
        name="attn_oproj",
    )(h, o, w_o.astype(BF16))


def kernel(x, p, g_mix_a, w_pool, b_pool, ls_pool, g_kv, w_kv, g_mix_b, w_q, w_o, g_moe, w_router_group, w_router_expert, w_gate_up, w_down, g_ple, w_ple_gate, w_ple_proj, g_final):
    B, S, D = x.shape
    N = B * S
    assert p.shape[0] == 2 and g_mix_a.shape[0] == 1 and g_mix_b.shape[0] == 1
    p = p.reshape(p.shape[0], N, PLE_DIM)

    h = _pool_mixer(x, g_mix_a[0], w_pool[0], b_pool[0].reshape(-1), ls_pool[0]).reshape(N, D)
    route, pos, ys = _moe_rows(h, g_moe[0], w_router_group[0], w_router_expert[0],
                               w_gate_up[0], w_down[0])
    h, q, kv = _ple_qkv(h, route, pos, ys, p[0], g_ple[0], w_ple_gate[0], w_ple_proj[0],
                        g_kv, w_kv, g_mix_b[0], w_q[0])

    o = _attention(q.reshape(B, S, D), kv.reshape(B, S, 2 * D)).reshape(N, D)
    h = _oproj(h, o, w_o[0])
    route, pos, ys = _moe_rows(h, g_moe[1], w_router_group[1], w_router_expert[1],
                               w_gate_up[1], w_down[1])
    out = _ple_final(h, route, pos, ys, p[1], g_ple[1], w_ple_gate[1], w_ple_proj[1], g_final)
    return out.reshape(B, S, D)
```

```python
import functools
import math

import jax
import jax.numpy as jnp
from jax import lax
from jax.experimental import pallas as pl
from jax.experimental.pallas import tpu as pltpu

D_MODEL = 1024
POOL_WINDOWS = (2, 4, 8, 16)
POOL_GROUP_DIM = D_MODEL // len(POOL_WINDOWS)
POOL_HALO = 16
HEAD_DIM = 64
N_GROUPS = 4
EXPERTS_PER_GROUP = 8
N_EXPERTS = N_GROUPS * EXPERTS_PER_GROUP
PAIRS_PER_GROUP = EXPERTS_PER_GROUP * (EXPERTS_PER_GROUP - 1) // 2
N_BUCKETS = N_GROUPS * PAIRS_PER_GROUP
D_EXPERT = D_MODEL // 4
PLE_DIM = 256
RMS_EPS = 1e-6
LANES = 128
ROUTER_LANES = LANES
SUBROWS = D_MODEL // LANES
DMA_UNROLL = 8
SUFFIX_CUTOFF = 106.0

F32 = jnp.float32
BF16 = jnp.bfloat16

assert N_BUCKETS <= ROUTER_LANES


def _vmem_limit(mib):
    return pltpu.CompilerParams(vmem_limit_bytes=mib * 1024 * 1024)


def _rms_scale(x):
    return lax.rsqrt(jnp.mean(x * x, axis=-1, keepdims=True) + RMS_EPS)


def _split_bf16(x):
    hi = x.astype(BF16)
    lo = (x - hi.astype(F32)).astype(BF16)
    return hi, lo


def _dot(a, b):
    return jnp.dot(a, b, preferred_element_type=F32)


def _pool_kernel(x_ref, xprev_ref, g_ref, w_ref, b_ref, ls_ref, o_ref, *, ts):
    i = pl.program_id(1)
    xc = x_ref[0]
    xp = xprev_ref[0]
    g = g_ref[...]
    xn_c = xc * _rms_scale(xc) * g
    xn_p = xp * _rms_scale(xp) * g * (i > 0).astype(F32)
    full = jnp.concatenate([xn_p, xn_c], axis=0)
    t_glob = i * ts + lax.broadcasted_iota(jnp.int32, (ts, 1), 0)
    ys = []
    for gi, w in enumerate(POOL_WINDOWS):
        f = full[:, gi * POOL_GROUP_DIM:(gi + 1) * POOL_GROUP_DIM]
        s = f
        k = 1
        while k < w:
            s = s + pltpu.roll(s, k, axis=0)
            k *= 2
        cnt = jnp.minimum(t_glob + 1, w).astype(F32)
        pooled = s[POOL_HALO:] / cnt - f[POOL_HALO:]
        ys.append(_dot(pooled.astype(BF16), w_ref[gi]))
    y = jnp.concatenate(ys, axis=1) + b_ref[...]
    o_ref[0] = xc + y * ls_ref[...]


def _pool_mixer(x, g, w_pool, b_pool, ls_pool, *, ts=512):
    B, S, D = x.shape
    row = lambda v: v.reshape(1, D)
    const = lambda shape: pl.BlockSpec(shape, lambda b, i: (0,) * len(shape))
    return pl.pallas_call(
        functools.partial(_pool_kernel, ts=ts),
        grid=(B, S // ts),
        in_specs=[
            pl.BlockSpec((1, ts, D), lambda b, i: (b, i, 0)),
            pl.BlockSpec((1, POOL_HALO, D),
                         lambda b, i: (b, jnp.maximum(i * (ts // POOL_HALO) - 1, 0), 0)),
            const((1, D)),
            const(w_pool.shape),
            const((1, D)),
            const((1, D)),
        ],
        out_specs=pl.BlockSpec((1, ts, D), lambda b, i: (b, i, 0)),
        out_shape=jax.ShapeDtypeStruct((B, S, D), F32),
        compiler_params=_vmem_limit(40),
        name="pool_mixer",
    )(x, x, row(g), w_pool.astype(BF16), row(b_pool), row(ls_pool))


def _to_rows(ref, x, n):
    for c in range(SUBROWS):
        ref[pl.ds(c, n, stride=SUBROWS), :] = x[:, c * LANES:(c + 1) * LANES]


def _from_rows(ref, n):
    return jnp.concatenate(
        [ref[pl.ds(c, n, stride=SUBROWS), :] for c in range(SUBROWS)], axis=1)


def _bits(x):
    return lax.bitcast_convert_type(x, jnp.uint32)


def _top2(logits):
    t = logits.shape[0]
    lane = lax.broadcasted_iota(jnp.int32, (t, ROUTER_LANES), 1).astype(F32)
    neg = -jnp.inf
    big = float(ROUTER_LANES)
    gl = jnp.where(lane < N_GROUPS, logits, neg)
    gmax = jnp.max(gl, axis=1, keepdims=True)
    g_top_p = 1.0 / jnp.sum(jnp.exp(gl - gmax), axis=1, keepdims=True)
    g_idx = jnp.min(jnp.where(gl == gmax, lane, big), axis=1, keepdims=True)
    lo = N_GROUPS + EXPERTS_PER_GROUP * g_idx
    el = jnp.where((lane >= lo) & (lane < lo + EXPERTS_PER_GROUP), logits, neg)
    v1 = jnp.max(el, axis=1, keepdims=True)
    i1 = jnp.min(jnp.where(el == v1, lane, big), axis=1, keepdims=True)
    el2 = jnp.where(lane == i1, neg, el)
    v2 = jnp.max(el2, axis=1, keepdims=True)
    i2 = jnp.min(jnp.where(el2 == v2, lane, big), axis=1, keepdims=True)
    ex = jnp.exp(v2 - v1)
    return g_idx, i1 - N_GROUPS, i2 - N_GROUPS, g_top_p / (1.0 + ex), g_top_p * ex / (1.0 + ex)


def _route_kernel(h_ref, g_ref, wrh_ref, wrl_ref, route_ref, cnt_ref, run_ref, *, tm):
    @pl.when(pl.program_id(0) == 0)
    def _():
        run_ref[...] = jnp.zeros_like(run_ref)

    x = h_ref[...]
    xh, xl = _split_bf16(x * _rms_scale(x) * g_ref[...])
    wrh = wrh_ref[...]
    grp, e1, e2, w1, w2 = _top2(_dot(xh, wrh) + _dot(xl, wrh) + _dot(xh, wrl_ref[...]))
    lo = jnp.minimum(e1, e2) - EXPERTS_PER_GROUP * grp
    hi = jnp.maximum(e1, e2) - EXPERTS_PER_GROUP * grp
    pair = lo * (2 * EXPERTS_PER_GROUP - 1 - lo) * 0.5 + (hi - lo - 1.0)
    bucket = PAIRS_PER_GROUP * grp + pair
    lane = lax.broadcasted_iota(jnp.int32, (tm, ROUTER_LANES), 1).astype(F32)
    mine = lane == bucket
    onehot = jnp.where(mine, 1.0, 0.0)
    earlier = (lax.broadcasted_iota(jnp.int32, (tm, tm), 0)
               > lax.broadcasted_iota(jnp.int32, (tm, tm), 1)).astype(BF16)
    before = _dot(earlier, onehot.astype(BF16)) + run_ref[...]
    rank = jnp.sum(jnp.where(mine, before, 0.0), axis=1, keepdims=True)
    run_ref[...] += jnp.sum(onehot, axis=0, keepdims=True)
    cnt_ref[...] = run_ref[...]
    first = e1 < e2
    fields = (bucket, rank, jnp.where(first, w1, w2), jnp.where(first, w2, w1))
    out = jnp.zeros((tm, ROUTER_LANES), F32)
    for k, f in enumerate(fields):
        out = jnp.where(lane == k, f, out)
    route_ref[...] = out


def _route(h, g, w_rg, w_re, *, tm=512):
    N, D = h.shape
    w_r = jnp.concatenate([w_rg, w_re], axis=1)
    w_r = jnp.pad(w_r, ((0, 0), (0, ROUTER_LANES - w_r.shape[1])))
    wrh, wrl = _split_bf16(w_r)
    const = lambda a, b: pl.BlockSpec((a, b), lambda i: (0, 0))
    return pl.pallas_call(
        functools.partial(_route_kernel, tm=tm),
        grid=(N // tm,),
        in_specs=[pl.BlockSpec((tm, D), lambda i: (i, 0)), const(1, D),
                  const(D, ROUTER_LANES), const(D, ROUTER_LANES)],
        out_specs=[pl.BlockSpec((tm, ROUTER_LANES), lambda i: (i, 0)), const(1, ROUTER_LANES)],
        out_shape=[jax.ShapeDtypeStruct((N, ROUTER_LANES), F32),
                   jax.ShapeDtypeStruct((1, ROUTER_LANES), F32)],
        scratch_shapes=[pltpu.VMEM((1, ROUTER_LANES), F32)],
        compiler_params=_vmem_limit(32),
        name="moe_route",
    )(h, g.reshape(1, D), wrh, wrl)


def _bucket_experts():
    lo, hi = [], []
    for g in range(N_GROUPS):
        for a in range(EXPERTS_PER_GROUP):
            for b in range(a + 1, EXPERTS_PER_GROUP):
                lo.append(g * EXPERTS_PER_GROUP + a)
                hi.append(g * EXPERTS_PER_GROUP + b)
    return jnp.array(lo, jnp.int32), jnp.array(hi, jnp.int32)


def _dispatch_plan(route, counts, tm, n_tiles):
    bucket = route[:, 0].astype(jnp.int32)
    rank = route[:, 1].astype(jnp.int32)
    cnt = counts[0, :N_BUCKETS].astype(jnp.int32)
    tiles = (cnt + tm - 1) // tm
    tile_end = jnp.cumsum(tiles)
    ids = jnp.arange(N_BUCKETS, dtype=jnp.int32)
    first_tile = jnp.sum(jnp.where(bucket[:, None] == ids[None, :],
                                   (tile_end - tiles)[None, :], 0), axis=1)
    t = jnp.arange(n_tiles, dtype=jnp.int32)
    tile_bucket = jnp.sum((t[:, None] >= tile_end[None, :]).astype(jnp.int32), axis=1)
    tile_bucket = jnp.minimum(tile_bucket, N_BUCKETS - 1)
    lo, hi = _bucket_experts()
    return (first_tile * tm + rank, jnp.take(lo, tile_bucket), jnp.take(hi, tile_bucket),
            tile_end[-1:].astype(jnp.int32))


def _row_copy(src, src_row, dst, dst_row, sem):
    return pltpu.make_async_copy(
        src.at[pl.ds(pl.multiple_of(src_row * SUBROWS, SUBROWS), SUBROWS), :],
        dst.at[pl.ds(pl.multiple_of(dst_row * SUBROWS, SUBROWS), SUBROWS), :], sem)


def _dispatch_kernel(pos_ref, h_ref, g_ref, xs_in, xs_out, stage, sem, *, tm):
    del xs_in
    i = pl.program_id(0)
    n = pl.num_programs(0)
    slot = i % 2

    def drain(s):
        pltpu.make_async_copy(stage.at[s], xs_out.at[pl.ds(0, tm * SUBROWS), :],
                              sem.at[s]).wait()

    @pl.when(i >= 2)
    def _():
        drain(slot)

    x = h_ref[...]
    buf = stage.at[slot]
    _to_rows(buf, _bits(x * _rms_scale(x) * g_ref[...]), tm)

    def issue(jj, _):
        for u in range(DMA_UNROLL):
            j = jj * DMA_UNROLL + u
            _row_copy(buf, j, xs_out, pos_ref[i * tm + j], sem.at[slot]).start()
        return 0

    lax.fori_loop(0, tm // DMA_UNROLL, issue, 0)

    @pl.when(i == n - 1)
    def _():
        drain(slot)

        @pl.when(n > 1)
        def _():
            drain(1 - slot)


def _dispatch(h, g, pos, n_rows, *, tm=256):
    N, D = h.shape
    grid_spec = pltpu.PrefetchScalarGridSpec(
        num_scalar_prefetch=1,
        grid=(N // tm,),
        in_specs=[pl.BlockSpec((tm, D), lambda i, pos: (i, 0)),
                  pl.BlockSpec((1, D), lambda i, pos: (0, 0)),
                  pl.BlockSpec(memory_space=pl.ANY)],
        out_specs=pl.BlockSpec(memory_space=pl.ANY),
        scratch_shapes=[pltpu.VMEM((2, tm * SUBROWS, LANES), jnp.uint32),
                        pltpu.SemaphoreType.DMA((2,))],
    )
    return pl.pallas_call(
        functools.partial(_dispatch_kernel, tm=tm),
        grid_spec=grid_spec,
        out_shape=jax.ShapeDtypeStruct((n_rows * SUBROWS, LANES), jnp.uint32),
        input_output_aliases={3: 0},
        compiler_params=_vmem_limit(32),
        name="moe_dispatch",
    )(pos, h, g.reshape(1, D), jnp.zeros((n_rows * SUBROWS, LANES), jnp.uint32))


def _experts_kernel(ta_ref, tb_ref, nu_ref, x_ref, wgu_a, wd_a, wgu_b, wd_b, y_ref, *, tm):
    del ta_ref, tb_ref

    @pl.when(pl.program_id(0) < nu_ref[0])
    def _():
        x = lax.bitcast_convert_type(_from_rows(x_ref, tm), F32).astype(BF16)

        def expert(wgu_ref, wd_ref):
            gu = _dot(x, wgu_ref[0, 0])
            gate = gu[:, :D_EXPERT]
            hmid = gate * jax.nn.sigmoid(gate) * gu[:, D_EXPERT:]
            y = _dot(hmid.astype(BF16), wd_ref[0, 0])
            return _bits(y.astype(BF16).astype(F32))

        _to_rows(y_ref, (expert(wgu_a, wd_a) >> 16) | expert(wgu_b, wd_b), tm)


def _experts(xs, tile_a, tile_b, n_used, w_gu, w_d, layer, *, tm):
    n_tiles = tile_a.shape[0]
    D = D_MODEL
    last = lambda i, nu: jnp.minimum(i, nu[0] - 1)
    rows = pl.BlockSpec((tm * SUBROWS, LANES), lambda i, ta, tb, nu: (last(i, nu), 0))
    gu = lambda which: pl.BlockSpec(
        (1, 1, D, 2 * D_EXPERT),
        lambda i, ta, tb, nu: (layer, (ta, tb)[which][last(i, nu)], 0, 0))
    dn = lambda which: pl.BlockSpec(
        (1, 1, D_EXPERT, D),
        lambda i, ta, tb, nu: (layer, (ta, tb)[which][last(i, nu)], 0, 0))
    grid_spec = pltpu.PrefetchScalarGridSpec(
        num_scalar_prefetch=3,
        grid=(n_tiles,),
        in_specs=[rows, gu(0), dn(0), gu(1), dn(1)],
        out_specs=rows,
    )
    return pl.pallas_call(
        functools.partial(_experts_kernel, tm=tm),
        grid_spec=grid_spec,
        out_shape=jax.ShapeDtypeStruct(xs.shape, jnp.uint32),
        input_output_aliases={3: 0},
        compiler_params=_vmem_limit(40),
        name="moe_experts",
    )(tile_a, tile_b, n_used, xs, w_gu, w_d, w_gu, w_d)


def _moe_rows(h, g, w_rg, w_re, w_gu, w_d, layer, *, tm=256):
    N = h.shape[0]
    n_tiles = N // tm + N_BUCKETS
    route, counts = _route(h, g, w_rg, w_re)
    pos, tile_a, tile_b, n_used = _dispatch_plan(route, counts, tm, n_tiles)
    xs = _dispatch(h, g, pos, n_tiles * tm)
    return route, pos, _experts(xs, tile_a, tile_b, n_used, w_gu, w_d, layer, tm=tm)


def _combine_ple(pos_ref, h_ref, route_ref, p_ref, g_ref, wg_ref, wp_ref, ys_hbm, buf, sem, *, tm):
    i = pl.program_id(0)
    n = pl.num_programs(0)
    slot = i % 2

    def fetch(step, s):
        def issue(jj, _):
            for u in range(DMA_UNROLL):
                j = jj * DMA_UNROLL + u
                _row_copy(ys_hbm, pos_ref[step * tm + j], buf.at[s], j, sem.at[s]).start()
            return 0
        lax.fori_loop(0, tm // DMA_UNROLL, issue, 0)

    @pl.when(i == 0)
    def _():
        fetch(0, 0)

    @pl.when(i + 1 < n)
    def _():
        fetch(i + 1, 1 - slot)

    rows = buf.at[slot]
    pltpu.make_async_copy(ys_hbm.at[pl.ds(0, tm * SUBROWS), :], rows, sem.at[slot]).wait()
    packed = _from_rows(rows, tm)
    y_lo = lax.bitcast_convert_type(packed << 16, F32)
    y_hi = lax.bitcast_convert_type(packed & jnp.uint32(0xFFFF0000), F32)
    r = route_ref[...]
    h = h_ref[...] + r[:, 2:3] * y_lo + r[:, 3:4] * y_hi
    xn = (h * _rms_scale(h) * g_ref[...]).astype(BF16)
    gate = jax.nn.sigmoid(_dot(xn, wg_ref[...]))
    proj = _dot(p_ref[0].astype(BF16), wp_ref[...])
    return h + proj * gate


def _ple_qkv_kernel(pos_ref, h_ref, route_ref, p_ref, g_ref, wg_ref, wp_ref, gkv_ref, wkv_ref,
                    gq_ref, wq_ref, ys_hbm, ho_ref, q_ref, kv_ref, buf, sem, *, tm):
    hn = _combine_ple(pos_ref, h_ref, route_ref, p_ref, g_ref, wg_ref, wp_ref, ys_hbm, buf, sem,
                      tm=tm)
    ho_ref[...] = hn
    base = hn * _rms_scale(hn)
    kv_ref[...] = _dot((base * gkv_ref[...]).astype(BF16), wkv_ref[...]).astype(BF16)
    q = _dot((base * gq_ref[...]).astype(BF16), wq_ref[...])
    q_ref[...] = (q * (1.0 / math.sqrt(HEAD_DIM))).astype(BF16)


def _ple_final_kernel(pos_ref, h_ref, route_ref, p_ref, g_ref, wg_ref, wp_ref, gf_ref, ys_hbm,
                      o_ref, buf, sem, *, tm):
    hn = _combine_ple(pos_ref, h_ref, route_ref, p_ref, g_ref, wg_ref, wp_ref, ys_hbm, buf, sem,
                      tm=tm)
    o_ref[...] = hn * _rms_scale(hn) * gf_ref[...]


def _ple_call(body, name, pos, h, route, p, layer, const_inputs, ys, out_widths, out_dtypes, *,
              tm=512):
    N = h.shape[0]
    tok = lambda w: pl.BlockSpec((tm, w), lambda i, pos: (i, 0))
    const = lambda a: pl.BlockSpec(a.shape, lambda i, pos: (0, 0))
    grid_spec = pltpu.PrefetchScalarGridSpec(
        num_scalar_prefetch=1,
        grid=(N // tm,),
        in_specs=([tok(h.shape[1]), tok(route.shape[1]),
                   pl.BlockSpec((1, tm, p.shape[2]), lambda i, pos: (layer, i, 0))]
                  + [const(a) for a in const_inputs] + [pl.BlockSpec(memory_space=pl.ANY)]),
        out_specs=[tok(w) for w in out_widths],
        scratch_shapes=[pltpu.VMEM((2, tm * SUBROWS, LANES), jnp.uint32),
                        pltpu.SemaphoreType.DMA((2,))],
    )
    return pl.pallas_call(
        functools.partial(body, tm=tm),
        grid_spec=grid_spec,
        out_shape=[jax.ShapeDtypeStruct((N, w), dt) for w, dt in zip(out_widths, out_dtypes)],
        compiler_params=_vmem_limit(48),
        name=name,
    )(pos, h, route, p, *const_inputs, ys)


def _ple_qkv(h, route, pos, ys, p, layer, g_ple, w_gate, w_proj, g_kv, w_kv, g_q, w_q):
    D = h.shape[1]
    row = lambda v: v.reshape(1, D)
    consts = [row(g_ple), w_gate.astype(BF16), w_proj.astype(BF16),
              row(g_kv), w_kv.astype(BF16), row(g_q), w_q.astype(BF16)]
    return _ple_call(_ple_qkv_kernel, "ple_qkv", pos, h, route, p, layer, consts, ys,
                     (D, D, 2 * D), (F32, BF16, BF16))


def _ple_final(h, route, pos, ys, p, layer, g_ple, w_gate, w_proj, g_final):
    D = h.shape[1]
    row = lambda v: v.reshape(1, D)
    consts = [row(g_ple), w_gate.astype(BF16), w_proj.astype(BF16), row(g_final)]
    return _ple_call(_ple_final_kernel, "ple_final", pos, h, route, p, layer, consts, ys,
                     (D,), (F32,))[0]


def _attn_kernel(q_ref, k_ref, v_ref, o_ref, *, tq, win, chunk, unroll):
    c = pl.program_id(2)
    nrow = 2 * tq
    lane = lax.broadcasted_iota(jnp.int32, (1, LANES), 1)
    even = lane < HEAD_DIM
    rows = lax.broadcasted_iota(jnp.int32, (nrow, win), 0)
    cols = lax.broadcasted_iota(jnp.int32, (nrow, win), 1)
    diag = cols - jnp.where(rows >= tq, rows - tq, rows)
    kj = lax.broadcasted_iota(jnp.int32, (win, win), 0)
    ks = lax.broadcasted_iota(jnp.int32, (win, win), 1)
    tri = (kj > ks).astype(BF16)

    def window(q2, ws, valid, carry):
        ws = pl.multiple_of(ws, tq)
        kb = k_ref[0, pl.ds(ws, win), :]
        vb = v_ref[0, pl.ds(ws, win), :]
        z = lax.dot_general(q2, kb, (((1,), (1,)), ((), ())), preferred_element_type=F32)
        sp = jnp.maximum(z, 0.0) + jnp.log(1.0 + jnp.exp(-jnp.abs(z)))
        logsig = z - sp
        sp = jnp.where(valid, sp, 0.0)
        suffix = _dot(sp.astype(BF16), tri)
        if carry is not None:
            suffix = suffix + carry
        a = jnp.where(valid, jnp.exp(logsig - suffix), 0.0)
        return _dot(a.astype(BF16), vb), suffix[:, :1] + sp[:, :1]

    def group(ig, _):
        q2s, ws0, outs, totals = [], [], [], []
        for u in range(unroll):
            r0 = pl.multiple_of((ig * unroll + u) * tq, tq)
            q0 = c * chunk + r0
            qb = q_ref[0, pl.ds(r0, tq), :]
            zero = jnp.zeros_like(qb)
            q2 = jnp.concatenate([jnp.where(even, qb, zero), jnp.where(even, zero, qb)], axis=0)
            ws = jnp.maximum(q0 + tq - win, 0)
            o, total = window(q2, ws, diag < q0 - ws, None)
            q2s.append(q2)
            ws0.append(ws)
            outs.append(o)
            totals.append(total)

        def least_mass(ts):
            return jnp.min(functools.reduce(jnp.minimum, ts))

        def cond(st):
            w, low, _, _ = st
            return jnp.logical_and(ws0[-1] - (w - 1) * win > 0, low < SUFFIX_CUTOFF)

        def body(st):
            w, _, outs, totals = st
            new_o, new_t = [], []
            for u in range(unroll):
                upper = jnp.maximum(ws0[u] - (w - 1) * win, 0)
                ws = jnp.maximum(upper - win, 0)
                o, total = window(q2s[u], ws, cols < upper - ws, totals[u])
                new_o.append(outs[u] + o)
                new_t.append(total)
            return w + 1, least_mass(new_t), tuple(new_o), tuple(new_t)

        _, _, outs, _ = lax.while_loop(
            cond, body, (jnp.int32(1), least_mass(totals), tuple(outs), tuple(totals)))
        for u in range(unroll):
            r0 = pl.multiple_of((ig * unroll + u) * tq, tq)
            o = outs[u]
            o_ref[0, pl.ds(r0, tq), :] = jnp.where(even, o[:tq], o[tq:]).astype(o_ref.dtype)
        return 0

    lax.fori_loop(0, chunk // (tq * unroll), group, 0)


def _attention(q, kv, *, tq=64, win=256, chunk=1024, unroll=4):
    B, S, D = q.shape
    n_pairs = D // LANES
    return pl.pallas_call(
        functools.partial(_attn_kernel, tq=tq, win=win, chunk=chunk, unroll=unroll),
        grid=(B, n_pairs, S // chunk),
        in_specs=[
            pl.BlockSpec((1, chunk, LANES), lambda b, hp, c: (b, c, hp)),
            pl.BlockSpec((1, S, LANES), lambda b, hp, c: (b, 0, hp)),
            pl.BlockSpec((1, S, LANES), lambda b, hp, c: (b, 0, n_pairs + hp)),
        ],
        out_specs=pl.BlockSpec((1, chunk, LANES), lambda b, hp, c: (b, c, hp)),
        out_shape=jax.ShapeDtypeStruct((B, S, D), BF16),
        compiler_params=_vmem_limit(48),
        name="stickbreak_attn",
    )(q, kv, kv)


def _oproj_kernel(h_ref, o_ref, w_ref, out_ref):
    out_ref[...] = h_ref[...] + _dot(o_ref[...], w_ref[...])


def _oproj(h, o, w_o, *, tm=1024):
    N, D = h.shape
    tok = pl.BlockSpec((tm, D), lambda i: (i, 0))
    return pl.pallas_call(
        _oproj_kernel,
        grid=(N // tm,),
        in_specs=[tok, tok, pl.BlockSpec((D, D), lambda i: (0, 0))],
        out_specs=tok,
        out_shape=jax.ShapeDtypeStruct((N, D), F32),
        compiler_params=_vmem_limit(40),
        name="attn_oproj",
    )(h, o, w_o.astype(BF16))


def kernel(x, p, g_mix_a, w_pool, b_pool, ls_pool, g_kv, w_kv, g_mix_b, w_q, w_o, g_moe, w_router_group, w_router_expert, w_gate_up, w_down, g_ple, w_ple_gate, w_ple_proj, g_final):
    B, S, D = x.shape
    N = B * S
    assert p.shape[0] == 2 and g_mix_a.shape[0] == 1 and g_mix_b.shape[0] == 1
    p = p.reshape(p.shape[0], N, PLE_DIM)
    w_gu = w_gate_up.astype(BF16)
    w_dn = w_down.astype(BF16)

    h = _pool_mixer(x, g_mix_a[0], w_pool[0], b_pool[0].reshape(-1), ls_pool[0]).reshape(N, D)
    route, pos, ys = _moe_rows(h, g_moe[0], w_router_group[0], w_router_expert[0], w_gu, w_dn, 0)
    h, q, kv = _ple_qkv(h, route, pos, ys, p, 0, g_ple[0], w_ple_gate[0], w_ple_proj[0],
                        g_kv, w_kv, g_mix_b[0], w_q[0])

    o = _attention(q.reshape(B, S, D), kv.reshape(B, S, 2 * D)).reshape(N, D)
    h = _oproj(h, o, w_o[0])
    route, pos, ys = _moe_rows(h, g_moe[1], w_router_group[1], w_router_expert[1], w_gu, w_dn, 1)
    out = _ple_final(h, route, pos, ys, p, 1, g_ple[1], w_ple_gate[1], w_ple_proj[1], g_final)
    return out.reshape(B, S, D)
```

```python
import functools
import math

import jax
import jax.numpy as jnp
from jax import lax
from jax.experimental import pallas as pl
from jax.experimental.pallas import tpu as pltpu

D_MODEL = 1024
POOL_WINDOWS = (2, 4, 8, 16)
POOL_GROUP_DIM = D_MODEL // len(POOL_WINDOWS)
POOL_HALO = 16
HEAD_DIM = 64
N_GROUPS = 4
EXPERTS_PER_GROUP = 8
N_EXPERTS = N_GROUPS * EXPERTS_PER_GROUP
PAIRS_PER_GROUP = EXPERTS_PER_GROUP * (EXPERTS_PER_GROUP - 1) // 2
N_BUCKETS = N_GROUPS * PAIRS_PER_GROUP
D_EXPERT = D_MODEL // 4
PLE_DIM = 256
RMS_EPS = 1e-6
LANES = 128
ROUTER_LANES = LANES
SUBROWS = D_MODEL // LANES
DMA_UNROLL = 8
QUERY_SCALE = math.log2(math.e) / math.sqrt(HEAD_DIM)
SUFFIX_CUTOFF = 153.0

F32 = jnp.float32
BF16 = jnp.bfloat16

assert N_BUCKETS <= ROUTER_LANES


def _vmem_limit(mib):
    return pltpu.CompilerParams(vmem_limit_bytes=mib * 1024 * 1024)


def _rms_scale(x):
    return lax.rsqrt(jnp.mean(x * x, axis=-1, keepdims=True) + RMS_EPS)


def _split_bf16(x):
    hi = x.astype(BF16)
    lo = (x - hi.astype(F32)).astype(BF16)
    return hi, lo


def _dot(a, b):
    return jnp.dot(a, b, preferred_element_type=F32)


def _pool_kernel(x_ref, xprev_ref, g_ref, w_ref, b_ref, ls_ref, o_ref, *, ts):
    i = pl.program_id(1)
    xc = x_ref[0]
    xp = xprev_ref[0]
    g = g_ref[...]
    xn_c = xc * _rms_scale(xc) * g
    xn_p = xp * _rms_scale(xp) * g * (i > 0).astype(F32)
    full = jnp.concatenate([xn_p, xn_c], axis=0)
    t_glob = i * ts + lax.broadcasted_iota(jnp.int32, (ts, 1), 0)
    ys = []
    for gi, w in enumerate(POOL_WINDOWS):
        f = full[:, gi * POOL_GROUP_DIM:(gi + 1) * POOL_GROUP_DIM]
        s = f
        k = 1
        while k < w:
            s = s + pltpu.roll(s, k, axis=0)
            k *= 2
        cnt = jnp.minimum(t_glob + 1, w).astype(F32)
        pooled = s[POOL_HALO:] / cnt - f[POOL_HALO:]
        ys.append(_dot(pooled.astype(BF16), w_ref[gi]))
    y = jnp.concatenate(ys, axis=1) + b_ref[...]
    o_ref[0] = xc + y * ls_ref[...]


def _pool_mixer(x, g, w_pool, b_pool, ls_pool, *, ts=512):
    B, S, D = x.shape
    row = lambda v: v.reshape(1, D)
    const = lambda shape: pl.BlockSpec(shape, lambda b, i: (0,) * len(shape))
    return pl.pallas_call(
        functools.partial(_pool_kernel, ts=ts),
        grid=(B, S // ts),
        in_specs=[
            pl.BlockSpec((1, ts, D), lambda b, i: (b, i, 0)),
            pl.BlockSpec((1, POOL_HALO, D),
                         lambda b, i: (b, jnp.maximum(i * (ts // POOL_HALO) - 1, 0), 0)),
            const((1, D)),
            const(w_pool.shape),
            const((1, D)),
            const((1, D)),
        ],
        out_specs=pl.BlockSpec((1, ts, D), lambda b, i: (b, i, 0)),
        out_shape=jax.ShapeDtypeStruct((B, S, D), F32),
        compiler_params=_vmem_limit(40),
        name="pool_mixer",
    )(x, x, row(g), w_pool.astype(BF16), row(b_pool), row(ls_pool))


def _to_rows(ref, x, n):
    for c in range(SUBROWS):
        ref[pl.ds(c, n, stride=SUBROWS), :] = x[:, c * LANES:(c + 1) * LANES]


def _from_rows(ref, n):
    return jnp.concatenate(
        [ref[pl.ds(c, n, stride=SUBROWS), :] for c in range(SUBROWS)], axis=1)


def _bits(x):
    return lax.bitcast_convert_type(x, jnp.uint32)


def _top2(logits):
    t = logits.shape[0]
    lane = lax.broadcasted_iota(jnp.int32, (t, ROUTER_LANES), 1).astype(F32)
    neg = -jnp.inf
    big = float(ROUTER_LANES)
    gl = jnp.where(lane < N_GROUPS, logits, neg)
    gmax = jnp.max(gl, axis=1, keepdims=True)
    g_top_p = 1.0 / jnp.sum(jnp.exp(gl - gmax), axis=1, keepdims=True)
    g_idx = jnp.min(jnp.where(gl == gmax, lane, big), axis=1, keepdims=True)
    lo = N_GROUPS + EXPERTS_PER_GROUP * g_idx
    el = jnp.where((lane >= lo) & (lane < lo + EXPERTS_PER_GROUP), logits, neg)
    v1 = jnp.max(el, axis=1, keepdims=True)
    i1 = jnp.min(jnp.where(el == v1, lane, big), axis=1, keepdims=True)
    el2 = jnp.where(lane == i1, neg, el)
    v2 = jnp.max(el2, axis=1, keepdims=True)
    i2 = jnp.min(jnp.where(el2 == v2, lane, big), axis=1, keepdims=True)
    ex = jnp.exp(v2 - v1)
    return g_idx, i1 - N_GROUPS, i2 - N_GROUPS, g_top_p / (1.0 + ex), g_top_p * ex / (1.0 + ex)


def _route_kernel(h_ref, g_ref, wrh_ref, wrl_ref, route_ref, cnt_ref, run_ref, *, tm):
    @pl.when(pl.program_id(0) == 0)
    def _():
        run_ref[...] = jnp.zeros_like(run_ref)

    x = h_ref[...]
    xh, xl = _split_bf16(x * _rms_scale(x) * g_ref[...])
    wrh = wrh_ref[...]
    grp, e1, e2, w1, w2 = _top2(_dot(xh, wrh) + _dot(xl, wrh) + _dot(xh, wrl_ref[...]))
    lo = jnp.minimum(e1, e2) - EXPERTS_PER_GROUP * grp
    hi = jnp.maximum(e1, e2) - EXPERTS_PER_GROUP * grp
    pair = lo * (2 * EXPERTS_PER_GROUP - 1 - lo) * 0.5 + (hi - lo - 1.0)
    bucket = PAIRS_PER_GROUP * grp + pair
    lane = lax.broadcasted_iota(jnp.int32, (tm, ROUTER_LANES), 1).astype(F32)
    mine = lane == bucket
    onehot = jnp.where(mine, 1.0, 0.0)
    earlier = (lax.broadcasted_iota(jnp.int32, (tm, tm), 0)
               > lax.broadcasted_iota(jnp.int32, (tm, tm), 1)).astype(BF16)
    before = _dot(earlier, onehot.astype(BF16)) + run_ref[...]
    rank = jnp.sum(jnp.where(mine, before, 0.0), axis=1, keepdims=True)
    run_ref[...] += jnp.sum(onehot, axis=0, keepdims=True)
    cnt_ref[...] = run_ref[...]
    first = e1 < e2
    fields = (bucket, rank, jnp.where(first, w1, w2), jnp.where(first, w2, w1))
    out = jnp.zeros((tm, ROUTER_LANES), F32)
    for k, f in enumerate(fields):
        out = jnp.where(lane == k, f, out)
    route_ref[...] = out


def _route(h, g, w_rg, w_re, *, tm=512):
    N, D = h.shape
    w_r = jnp.concatenate([w_rg, w_re], axis=1)
    w_r = jnp.pad(w_r, ((0, 0), (0, ROUTER_LANES - w_r.shape[1])))
    wrh, wrl = _split_bf16(w_r)
    const = lambda a, b: pl.BlockSpec((a, b), lambda i: (0, 0))
    return pl.pallas_call(
        functools.partial(_route_kernel, tm=tm),
        grid=(N // tm,),
        in_specs=[pl.BlockSpec((tm, D), lambda i: (i, 0)), const(1, D),
                  const(D, ROUTER_LANES), const(D, ROUTER_LANES)],
        out_specs=[pl.BlockSpec((tm, ROUTER_LANES), lambda i: (i, 0)), const(1, ROUTER_LANES)],
        out_shape=[jax.ShapeDtypeStruct((N, ROUTER_LANES), F32),
                   jax.ShapeDtypeStruct((1, ROUTER_LANES), F32)],
        scratch_shapes=[pltpu.VMEM((1, ROUTER_LANES), F32)],
        compiler_params=_vmem_limit(32),
        name="moe_route",
    )(h, g.reshape(1, D), wrh, wrl)


def _bucket_experts():
    lo, hi = [], []
    for g in range(N_GROUPS):
        for a in range(EXPERTS_PER_GROUP):
            for b in range(a + 1, EXPERTS_PER_GROUP):
                lo.append(g * EXPERTS_PER_GROUP + a)
                hi.append(g * EXPERTS_PER_GROUP + b)
    return jnp.array(lo, jnp.int32), jnp.array(hi, jnp.int32)


def _dispatch_plan(route, counts, tm, n_tiles):
    bucket = route[:, 0].astype(jnp.int32)
    rank = route[:, 1].astype(jnp.int32)
    cnt = counts[0, :N_BUCKETS].astype(jnp.int32)
    tiles = (cnt + tm - 1) // tm
    tile_end = jnp.cumsum(tiles)
    ids = jnp.arange(N_BUCKETS, dtype=jnp.int32)
    first_tile = jnp.sum(jnp.where(bucket[:, None] == ids[None, :],
                                   (tile_end - tiles)[None, :], 0), axis=1)
    t = jnp.arange(n_tiles, dtype=jnp.int32)
    tile_bucket = jnp.sum((t[:, None] >= tile_end[None, :]).astype(jnp.int32), axis=1)
    tile_bucket = jnp.minimum(tile_bucket, N_BUCKETS - 1)
    lo, hi = _bucket_experts()
    return (first_tile * tm + rank, jnp.take(lo, tile_bucket), jnp.take(hi, tile_bucket),
            tile_end[-1:].astype(jnp.int32))


def _row_copy(src, src_row, dst, dst_row, sem):
    return pltpu.make_async_copy(
        src.at[pl.ds(pl.multiple_of(src_row * SUBROWS, SUBROWS), SUBROWS), :],
        dst.at[pl.ds(pl.multiple_of(dst_row * SUBROWS, SUBROWS), SUBROWS), :], sem)


def _dispatch_kernel(pos_ref, h_ref, g_ref, xs_in, xs_out, stage, sem, *, tm):
    del xs_in
    i = pl.program_id(0)
    n = pl.num_programs(0)
    slot = i % 2

    def drain(s):
        pltpu.make_async_copy(stage.at[s], xs_out.at[pl.ds(0, tm * SUBROWS), :],
                              sem.at[s]).wait()

    @pl.when(i >= 2)
    def _():
        drain(slot)

    x = h_ref[...]
    buf = stage.at[slot]
    _to_rows(buf, _bits(x * _rms_scale(x) * g_ref[...]), tm)

    def issue(jj, _):
        for u in range(DMA_UNROLL):
            j = jj * DMA_UNROLL + u
            _row_copy(buf, j, xs_out, pos_ref[i * tm + j], sem.at[slot]).start()
        return 0

    lax.fori_loop(0, tm // DMA_UNROLL, issue, 0)

    @pl.when(i == n - 1)
    def _():
        drain(slot)

        @pl.when(n > 1)
        def _():
            drain(1 - slot)


def _dispatch(h, g, pos, n_rows, *, tm=256):
    N, D = h.shape
    grid_spec = pltpu.PrefetchScalarGridSpec(
        num_scalar_prefetch=1,
        grid=(N // tm,),
        in_specs=[pl.BlockSpec((tm, D), lambda i, pos: (i, 0)),
                  pl.BlockSpec((1, D), lambda i, pos: (0, 0)),
                  pl.BlockSpec(memory_space=pl.ANY)],
        out_specs=pl.BlockSpec(memory_space=pl.ANY),
        scratch_shapes=[pltpu.VMEM((2, tm * SUBROWS, LANES), jnp.uint32),
                        pltpu.SemaphoreType.DMA((2,))],
    )
    return pl.pallas_call(
        functools.partial(_dispatch_kernel, tm=tm),
        grid_spec=grid_spec,
        out_shape=jax.ShapeDtypeStruct((n_rows * SUBROWS, LANES), jnp.uint32),
        input_output_aliases={3: 0},
        compiler_params=_vmem_limit(32),
        name="moe_dispatch",
    )(pos, h, g.reshape(1, D), jnp.zeros((n_rows * SUBROWS, LANES), jnp.uint32))


def _experts_kernel(ta_ref, tb_ref, nu_ref, x_ref, wgu_a, wd_a, wgu_b, wd_b, y_ref, *, tm):
    del ta_ref, tb_ref

    @pl.when(pl.program_id(0) < nu_ref[0])
    def _():
        x = lax.bitcast_convert_type(_from_rows(x_ref, tm), F32).astype(BF16)

        def expert(wgu_ref, wd_ref):
            gu = _dot(x, wgu_ref[0, 0])
            gate = gu[:, :D_EXPERT]
            hmid = gate * jax.nn.sigmoid(gate) * gu[:, D_EXPERT:]
            y = _dot(hmid.astype(BF16), wd_ref[0, 0])
            return _bits(y.astype(BF16).astype(F32))

        _to_rows(y_ref, (expert(wgu_a, wd_a) >> 16) | expert(wgu_b, wd_b), tm)


def _experts(xs, tile_a, tile_b, n_used, w_gu, w_d, layer, *, tm):
    n_tiles = tile_a.shape[0]
    D = D_MODEL
    last = lambda i, nu: jnp.minimum(i, nu[0] - 1)
    rows = pl.BlockSpec((tm * SUBROWS, LANES), lambda i, ta, tb, nu: (last(i, nu), 0))
    gu = lambda which: pl.BlockSpec(
        (1, 1, D, 2 * D_EXPERT),
        lambda i, ta, tb, nu: (layer, (ta, tb)[which][last(i, nu)], 0, 0))
    dn = lambda which: pl.BlockSpec(
        (1, 1, D_EXPERT, D),
        lambda i, ta, tb, nu: (layer, (ta, tb)[which][last(i, nu)], 0, 0))
    grid_spec = pltpu.PrefetchScalarGridSpec(
        num_scalar_prefetch=3,
        grid=(n_tiles,),
        in_specs=[rows, gu(0), dn(0), gu(1), dn(1)],
        out_specs=rows,
    )
    return pl.pallas_call(
        functools.partial(_experts_kernel, tm=tm),
        grid_spec=grid_spec,
        out_shape=jax.ShapeDtypeStruct(xs.shape, jnp.uint32),
        input_output_aliases={3: 0},
        compiler_params=_vmem_limit(40),
        name="moe_experts",
    )(tile_a, tile_b, n_used, xs, w_gu, w_d, w_gu, w_d)


def _moe_rows(h, g, w_rg, w_re, w_gu, w_d, layer, *, tm=256):
    N = h.shape[0]
    n_tiles = N // tm + N_BUCKETS
    route, counts = _route(h, g, w_rg, w_re)
    pos, tile_a, tile_b, n_used = _dispatch_plan(route, counts, tm, n_tiles)
    xs = _dispatch(h, g, pos, n_tiles * tm)
    return route, pos, _experts(xs, tile_a, tile_b, n_used, w_gu, w_d, layer, tm=tm)


def _combine_ple(pos_ref, h_ref, route_ref, p_ref, g_ref, wg_ref, wp_ref, ys_hbm, buf, sem, *, tm):
    i = pl.program_id(0)
    n = pl.num_programs(0)
    slot = i % 2

    def wait(s):
        pltpu.make_async_copy(ys_hbm.at[pl.ds(0, tm * SUBROWS), :], buf.at[s], sem.at[s]).wait()

    @pl.when(i == 0)
    def _():
        def issue(jj, _):
            for u in range(DMA_UNROLL):
                j = jj * DMA_UNROLL + u
                _row_copy(ys_hbm, pos_ref[j], buf.at[0], j, sem.at[0]).start()
            return 0
        lax.fori_loop(0, tm // DMA_UNROLL, issue, 0)

    rows = buf.at[slot]
    wait(slot)
    nxt = jnp.minimum(i + 1, n - 1) * tm
    for j in range(tm):
        _row_copy(ys_hbm, pos_ref[nxt + j], buf.at[1 - slot], j, sem.at[1 - slot]).start()
    packed = _from_rows(rows, tm)
    y_lo = lax.bitcast_convert_type(packed << 16, F32)
    y_hi = lax.bitcast_convert_type(packed & jnp.uint32(0xFFFF0000), F32)
    r = route_ref[...]
    h = h_ref[...] + r[:, 2:3] * y_lo + r[:, 3:4] * y_hi
    xn = (h * _rms_scale(h) * g_ref[...]).astype(BF16)
    gate = jax.nn.sigmoid(_dot(xn, wg_ref[...]))
    proj = _dot(p_ref[0].astype(BF16), wp_ref[...])
    return h + proj * gate


def _drain_last_fetch(ys_hbm, buf, sem, tm):
    i = pl.program_id(0)

    @pl.when(i == pl.num_programs(0) - 1)
    def _():
        s = 1 - i % 2
        pltpu.make_async_copy(ys_hbm.at[pl.ds(0, tm * SUBROWS), :], buf.at[s], sem.at[s]).wait()


def _ple_qkv_kernel(pos_ref, h_ref, route_ref, p_ref, g_ref, wg_ref, wp_ref, gkv_ref, wkv_ref,
                    gq_ref, wq_ref, ys_hbm, ho_ref, q_ref, kv_ref, buf, sem, *, tm):
    hn = _combine_ple(pos_ref, h_ref, route_ref, p_ref, g_ref, wg_ref, wp_ref, ys_hbm, buf, sem,
                      tm=tm)
    ho_ref[...] = hn
    base = hn * _rms_scale(hn)
    kv_ref[...] = _dot((base * gkv_ref[...]).astype(BF16), wkv_ref[...]).astype(BF16)
    q = _dot((base * gq_ref[...]).astype(BF16), wq_ref[...])
    q_ref[...] = (q * QUERY_SCALE).astype(BF16)
    _drain_last_fetch(ys_hbm, buf, sem, tm)


def _ple_final_kernel(pos_ref, h_ref, route_ref, p_ref, g_ref, wg_ref, wp_ref, gf_ref, ys_hbm,
                      o_ref, buf, sem, *, tm):
    hn = _combine_ple(pos_ref, h_ref, route_ref, p_ref, g_ref, wg_ref, wp_ref, ys_hbm, buf, sem,
                      tm=tm)
    o_ref[...] = hn * _rms_scale(hn) * gf_ref[...]
    _drain_last_fetch(ys_hbm, buf, sem, tm)


def _ple_call(body, name, pos, h, route, p, layer, const_inputs, ys, out_widths, out_dtypes, *,
              tm=512):
    N = h.shape[0]
    tok = lambda w: pl.BlockSpec((tm, w), lambda i, pos: (i, 0))
    const = lambda a: pl.BlockSpec(a.shape, lambda i, pos: (0, 0))
    grid_spec = pltpu.PrefetchScalarGridSpec(
        num_scalar_prefetch=1,
        grid=(N // tm,),
        in_specs=([tok(h.shape[1]), tok(route.shape[1]),
                   pl.BlockSpec((1, tm, p.shape[2]), lambda i, pos: (layer, i, 0))]
                  + [const(a) for a in const_inputs] + [pl.BlockSpec(memory_space=pl.ANY)]),
        out_specs=[tok(w) for w in out_widths],
        scratch_shapes=[pltpu.VMEM((2, tm * SUBROWS, LANES), jnp.uint32),
                        pltpu.SemaphoreType.DMA((2,))],
    )
    return pl.pallas_call(
        functools.partial(body, tm=tm),
        grid_spec=grid_spec,
        out_shape=[jax.ShapeDtypeStruct((N, w), dt) for w, dt in zip(out_widths, out_dtypes)],
        compiler_params=_vmem_limit(48),
        name=name,
    )(pos, h, route, p, *const_inputs, ys)


def _ple_qkv(h, route, pos, ys, p, layer, g_ple, w_gate, w_proj, g_kv, w_kv, g_q, w_q):
    D = h.shape[1]
    row = lambda v: v.reshape(1, D)
    consts = [row(g_ple), w_gate.astype(BF16), w_proj.astype(BF16),
              row(g_kv), w_kv.astype(BF16), row(g_q), w_q.astype(BF16)]
    return _ple_call(_ple_qkv_kernel, "ple_qkv", pos, h, route, p, layer, consts, ys,
                     (D, D, 2 * D), (F32, BF16, BF16))


def _ple_final(h, route, pos, ys, p, layer, g_ple, w_gate, w_proj, g_final):
    D = h.shape[1]
    row = lambda v: v.reshape(1, D)
    consts = [row(g_ple), w_gate.astype(BF16), w_proj.astype(BF16), row(g_final)]
    return _ple_call(_ple_final_kernel, "ple_final", pos, h, route, p, layer, consts, ys,
                     (D,), (F32,))[0]


def _attn_kernel(q_ref, k_ref, v_ref, o_ref, *, tq, win, chunk, unroll):
    c = pl.program_id(2)
    nrow = 2 * tq
    lane = lax.broadcasted_iota(jnp.int32, (1, LANES), 1)
    even = lane < HEAD_DIM
    rows = lax.broadcasted_iota(jnp.int32, (nrow, win), 0)
    cols = lax.broadcasted_iota(jnp.int32, (nrow, win), 1)
    diag = cols - jnp.where(rows >= tq, rows - tq, rows)
    kj = lax.broadcasted_iota(jnp.int32, (win, win), 0)
    ks = lax.broadcasted_iota(jnp.int32, (win, win), 1)
    tri = (kj > ks).astype(BF16)

    def windows(q2s, wss, valids, carries):
        n = len(q2s)
        wss = [pl.multiple_of(ws, tq) for ws in wss]
        zs = [lax.dot_general(q2s[u], k_ref[0, pl.ds(wss[u], win), :], (((1,), (1,)), ((), ())),
                              preferred_element_type=F32) for u in range(n)]
        sps = [jnp.maximum(z, 0.0) + jnp.log2(1.0 + jnp.exp2(-jnp.abs(z))) for z in zs]
        logsigs = [z - sp for z, sp in zip(zs, sps)]
        sps = [jnp.where(valid, sp, 0.0) for valid, sp in zip(valids, sps)]
        suffixes = [_dot(sp.astype(BF16), tri) for sp in sps]
        if carries is not None:
            suffixes = [s + carry for s, carry in zip(suffixes, carries)]
        probs = [jnp.where(valid, jnp.exp2(ls - s), 0.0)
                 for valid, ls, s in zip(valids, logsigs, suffixes)]
        outs = [_dot(probs[u].astype(BF16), v_ref[0, pl.ds(wss[u], win), :]) for u in range(n)]
        return outs, [s[:, :1] + sp[:, :1] for s, sp in zip(suffixes, sps)]

    def group(ig, _):
        q2s, ws0, valids = [], [], []
        for u in range(unroll):
            r0 = pl.multiple_of((ig * unroll + u) * tq, tq)
            q0 = c * chunk + r0
            qb = q_ref[0, pl.ds(r0, tq), :]
            zero = jnp.zeros_like(qb)
            q2s.append(jnp.concatenate([jnp.where(even, qb, zero), jnp.where(even, zero, qb)],
                                       axis=0))
            ws0.append(jnp.maximum(q0 + tq - win, 0))
            valids.append(diag < q0 - ws0[u])
        outs, totals = windows(q2s, ws0, valids, None)

        def least_mass(ts):
            return jnp.min(functools.reduce(jnp.minimum, ts))

        def cond(st):
            w, low, _, _ = st
            return jnp.logical_and(ws0[-1] - (w - 1) * win > 0, low < SUFFIX_CUTOFF)

        def body(st):
            w, _, outs, totals = st
            uppers = [jnp.maximum(ws - (w - 1) * win, 0) for ws in ws0]
            wss = [jnp.maximum(upper - win, 0) for upper in uppers]
            more, new_t = windows(q2s, wss, [cols < upper - ws for upper, ws in zip(uppers, wss)],
                                  totals)
            return (w + 1, least_mass(new_t), tuple(o + m for o, m in zip(outs, more)),
                    tuple(new_t))

        _, _, outs, _ = lax.while_loop(
            cond, body, (jnp.int32(1), least_mass(totals), tuple(outs), tuple(totals)))
        for u in range(unroll):
            r0 = pl.multiple_of((ig * unroll + u) * tq, tq)
            o = outs[u]
            o_ref[0, pl.ds(r0, tq), :] = jnp.where(even, o[:tq], o[tq:]).astype(o_ref.dtype)
        return 0

    lax.fori_loop(0, chunk // (tq * unroll), group, 0)


def _attention(q, kv, *, tq=64, win=256, chunk=1024, unroll=8):
    B, S, D = q.shape
    n_pairs = D // LANES
    return pl.pallas_call(
        functools.partial(_attn_kernel, tq=tq, win=win, chunk=chunk, unroll=unroll),
        grid=(B, n_pairs, S // chunk),
        in_specs=[
            pl.BlockSpec((1, chunk, LANES), lambda b, hp, c: (b, c, hp)),
            pl.BlockSpec((1, S, LANES), lambda b, hp, c: (b, 0, hp)),
            pl.BlockSpec((1, S, LANES), lambda b, hp, c: (b, 0, n_pairs + hp)),
        ],
        out_specs=pl.BlockSpec((1, chunk, LANES), lambda b, hp, c: (b, c, hp)),
        out_shape=jax.ShapeDtypeStruct((B, S, D), BF16),
        compiler_params=_vmem_limit(48),
        name="stickbreak_attn",
    )(q, kv, kv)


def _oproj_kernel(h_ref, o_ref, w_ref, out_ref):
    out_ref[...] = h_ref[...] + _dot(o_ref[...], w_ref[...])


def _oproj(h, o, w_o, *, tm=1024):
    N, D = h.shape
    tok = pl.BlockSpec((tm, D), lambda i: (i, 0))
    return pl.pallas_call(
        _oproj_kernel,
        grid=(N // tm,),
        in_specs=[tok, tok, pl.BlockSpec((D, D), lambda i: (0, 0))],
        out_specs=tok,
        out_shape=jax.ShapeDtypeStruct((N, D), F32),
        compiler_params=_vmem_limit(40),
        name="attn_oproj",
    )(h, o, w_o.astype(BF16))


def kernel(x, p, g_mix_a, w_pool, b_pool, ls_pool, g_kv, w_kv, g_mix_b, w_q, w_o, g_moe, w_router_group, w_router_expert, w_gate_up, w_down, g_ple, w_ple_gate, w_ple_proj, g_final):
    B, S, D = x.shape
    N = B * S
    assert p.shape[0] == 2 and g_mix_a.shape[0] == 1 and g_mix_b.shape[0] == 1
    p = p.reshape(p.shape[0], N, PLE_DIM)
    w_gu = w_gate_up.astype(BF16)
    w_dn = w_down.astype(BF16)

    h = _pool_mixer(x, g_mix_a[0], w_pool[0], b_pool[0].reshape(-1), ls_pool[0]).reshape(N, D)
    route, pos, ys = _moe_rows(h, g_moe[0], w_router_group[0], w_router_expert[0], w_gu, w_dn, 0)
    h, q, kv = _ple_qkv(h, route, pos, ys, p, 0, g_ple[0], w_ple_gate[0], w_ple_proj[0],
                        g_kv, w_kv, g_mix_b[0], w_q[0])

    o = _attention(q.reshape(B, S, D), kv.reshape(B, S, 2 * D)).reshape(N, D)
    h = _oproj(h, o, w_o[0])
    route, pos, ys = _moe_rows(h, g_moe[1], w_router_group[1], w_router_expert[1], w_gu, w_dn, 1)
    out = _ple_final(h, route, pos, ys, p, 1, g_ple[1], w_ple_gate[1], w_ple_proj[1], g_final)
    return out.reshape(B, S, D)
```

```python
import functools
import math

import jax
import jax.numpy as jnp
from jax import lax
from jax.experimental import pallas as pl
from jax.experimental.pallas import tpu as pltpu

D_MODEL = 1024
POOL_WINDOWS = (2, 4, 8, 16)
POOL_GROUP_DIM = D_MODEL // len(POOL_WINDOWS)
POOL_HALO = 16
HEAD_DIM = 64
N_GROUPS = 4
EXPERTS_PER_GROUP = 8
N_EXPERTS = N_GROUPS * EXPERTS_PER_GROUP
PAIRS_PER_GROUP = EXPERTS_PER_GROUP * (EXPERTS_PER_GROUP - 1) // 2
N_BUCKETS = N_GROUPS * PAIRS_PER_GROUP
D_EXPERT = D_MODEL // 4
PLE_DIM = 256
RMS_EPS = 1e-6
LANES = 128
ROUTER_LANES = LANES
SUBROWS = D_MODEL // LANES
DMA_UNROLL = 8
EXPERT_TILE_ALIGN = 64
QUERY_SCALE = math.log2(math.e) / math.sqrt(HEAD_DIM)
SUFFIX_CUTOFF = 153.0
MASKED_SCORE = -1e30

F32 = jnp.float32
BF16 = jnp.bfloat16

assert N_BUCKETS <= ROUTER_LANES


def _vmem_limit(mib):
    return pltpu.CompilerParams(vmem_limit_bytes=mib * 1024 * 1024)


def _rms_scale(x):
    return lax.rsqrt(jnp.mean(x * x, axis=-1, keepdims=True) + RMS_EPS)


def _split_bf16(x):
    hi = x.astype(BF16)
    lo = (x - hi.astype(F32)).astype(BF16)
    return hi, lo


def _dot(a, b):
    return jnp.dot(a, b, preferred_element_type=F32)


def _pool_kernel(x_ref, xprev_ref, g_ref, w_ref, b_ref, ls_ref, o_ref, *, ts):
    i = pl.program_id(1)
    xc = x_ref[0]
    xp = xprev_ref[0]
    g = g_ref[...]
    xn_c = xc * _rms_scale(xc) * g
    xn_p = xp * _rms_scale(xp) * g * (i > 0).astype(F32)
    full = jnp.concatenate([xn_p, xn_c], axis=0)
    t_glob = i * ts + lax.broadcasted_iota(jnp.int32, (ts, 1), 0)
    ys = []
    for gi, w in enumerate(POOL_WINDOWS):
        f = full[:, gi * POOL_GROUP_DIM:(gi + 1) * POOL_GROUP_DIM]
        s = f
        k = 1
        while k < w:
            s = s + pltpu.roll(s, k, axis=0)
            k *= 2
        cnt = jnp.minimum(t_glob + 1, w).astype(F32)
        pooled = s[POOL_HALO:] / cnt - f[POOL_HALO:]
        ys.append(_dot(pooled.astype(BF16), w_ref[gi]))
    y = jnp.concatenate(ys, axis=1) + b_ref[...]
    o_ref[0] = xc + y * ls_ref[...]


def _pool_mixer(x, g, w_pool, b_pool, ls_pool, *, ts=512):
    B, S, D = x.shape
    row = lambda v: v.reshape(1, D)
    const = lambda shape: pl.BlockSpec(shape, lambda b, i: (0,) * len(shape))
    return pl.pallas_call(
        functools.partial(_pool_kernel, ts=ts),
        grid=(B, S // ts),
        in_specs=[
            pl.BlockSpec((1, ts, D), lambda b, i: (b, i, 0)),
            pl.BlockSpec((1, POOL_HALO, D),
                         lambda b, i: (b, jnp.maximum(i * (ts // POOL_HALO) - 1, 0), 0)),
            const((1, D)),
            const(w_pool.shape),
            const((1, D)),
            const((1, D)),
        ],
        out_specs=pl.BlockSpec((1, ts, D), lambda b, i: (b, i, 0)),
        out_shape=jax.ShapeDtypeStruct((B, S, D), F32),
        compiler_params=_vmem_limit(40),
        name="pool_mixer",
    )(x, x, row(g), w_pool.astype(BF16), row(b_pool), row(ls_pool))


def _to_rows(ref, x, n):
    for c in range(SUBROWS):
        ref[pl.ds(c, n, stride=SUBROWS), :] = x[:, c * LANES:(c + 1) * LANES]


def _from_rows(ref, n):
    return jnp.concatenate(
        [ref[pl.ds(c, n, stride=SUBROWS), :] for c in range(SUBROWS)], axis=1)


def _bits(x):
    return lax.bitcast_convert_type(x, jnp.uint32)


def _top2(logits):
    t = logits.shape[0]
    lane = lax.broadcasted_iota(jnp.int32, (t, ROUTER_LANES), 1).astype(F32)
    neg = -jnp.inf
    big = float(ROUTER_LANES)
    gl = jnp.where(lane < N_GROUPS, logits, neg)
    gmax = jnp.max(gl, axis=1, keepdims=True)
    g_top_p = 1.0 / jnp.sum(jnp.exp(gl - gmax), axis=1, keepdims=True)
    g_idx = jnp.min(jnp.where(gl == gmax, lane, big), axis=1, keepdims=True)
    lo = N_GROUPS + EXPERTS_PER_GROUP * g_idx
    el = jnp.where((lane >= lo) & (lane < lo + EXPERTS_PER_GROUP), logits, neg)
    v1 = jnp.max(el, axis=1, keepdims=True)
    i1 = jnp.min(jnp.where(el == v1, lane, big), axis=1, keepdims=True)
    el2 = jnp.where(lane == i1, neg, el)
    v2 = jnp.max(el2, axis=1, keepdims=True)
    i2 = jnp.min(jnp.where(el2 == v2, lane, big), axis=1, keepdims=True)
    ex = jnp.exp(v2 - v1)
    return g_idx, i1 - N_GROUPS, i2 - N_GROUPS, g_top_p / (1.0 + ex), g_top_p * ex / (1.0 + ex)


def _route_kernel(h_ref, g_ref, wrh_ref, wrl_ref, route_ref, cnt_ref, run_ref, *, tm):
    @pl.when(pl.program_id(0) == 0)
    def _():
        run_ref[...] = jnp.zeros_like(run_ref)

    x = h_ref[...]
    xh, xl = _split_bf16(x * _rms_scale(x) * g_ref[...])
    wrh = wrh_ref[...]
    grp, e1, e2, w1, w2 = _top2(_dot(xh, wrh) + _dot(xl, wrh) + _dot(xh, wrl_ref[...]))
    lo = jnp.minimum(e1, e2) - EXPERTS_PER_GROUP * grp
    hi = jnp.maximum(e1, e2) - EXPERTS_PER_GROUP * grp
    pair = lo * (2 * EXPERTS_PER_GROUP - 1 - lo) * 0.5 + (hi - lo - 1.0)
    bucket = PAIRS_PER_GROUP * grp + pair
    lane = lax.broadcasted_iota(jnp.int32, (tm, ROUTER_LANES), 1).astype(F32)
    mine = lane == bucket
    onehot = jnp.where(mine, 1.0, 0.0)
    earlier = (lax.broadcasted_iota(jnp.int32, (tm, tm), 0)
               > lax.broadcasted_iota(jnp.int32, (tm, tm), 1)).astype(BF16)
    before = _dot(earlier, onehot.astype(BF16)) + run_ref[...]
    rank = jnp.sum(jnp.where(mine, before, 0.0), axis=1, keepdims=True)
    run_ref[...] += jnp.sum(onehot, axis=0, keepdims=True)
    cnt_ref[...] = run_ref[...]
    first = e1 < e2
    fields = (bucket, rank, jnp.where(first, w1, w2), jnp.where(first, w2, w1))
    out = jnp.zeros((tm, ROUTER_LANES), F32)
    for k, f in enumerate(fields):
        out = jnp.where(lane == k, f, out)
    route_ref[...] = out


def _route(h, g, w_rg, w_re, *, tm=512):
    N, D = h.shape
    w_r = jnp.concatenate([w_rg, w_re], axis=1)
    w_r = jnp.pad(w_r, ((0, 0), (0, ROUTER_LANES - w_r.shape[1])))
    wrh, wrl = _split_bf16(w_r)
    const = lambda a, b: pl.BlockSpec((a, b), lambda i: (0, 0))
    return pl.pallas_call(
        functools.partial(_route_kernel, tm=tm),
        grid=(N // tm,),
        in_specs=[pl.BlockSpec((tm, D), lambda i: (i, 0)), const(1, D),
                  const(D, ROUTER_LANES), const(D, ROUTER_LANES)],
        out_specs=[pl.BlockSpec((tm, ROUTER_LANES), lambda i: (i, 0)), const(1, ROUTER_LANES)],
        out_shape=[jax.ShapeDtypeStruct((N, ROUTER_LANES), F32),
                   jax.ShapeDtypeStruct((1, ROUTER_LANES), F32)],
        scratch_shapes=[pltpu.VMEM((1, ROUTER_LANES), F32)],
        compiler_params=_vmem_limit(32),
        name="moe_route",
    )(h, g.reshape(1, D), wrh, wrl)


def _bucket_experts():
    lo, hi = [], []
    for g in range(N_GROUPS):
        for a in range(EXPERTS_PER_GROUP):
            for b in range(a + 1, EXPERTS_PER_GROUP):
                lo.append(g * EXPERTS_PER_GROUP + a)
                hi.append(g * EXPERTS_PER_GROUP + b)
    return jnp.array(lo, jnp.int32), jnp.array(hi, jnp.int32)


def _dispatch_plan(route, counts, tm, n_tiles):
    bucket = route[:, 0].astype(jnp.int32)
    rank = route[:, 1].astype(jnp.int32)
    cnt = counts[0, :N_BUCKETS].astype(jnp.int32)
    tiles = (cnt + tm - 1) // tm
    tile_end = jnp.cumsum(tiles)
    ids = jnp.arange(N_BUCKETS, dtype=jnp.int32)
    first_tile = jnp.sum(jnp.where(bucket[:, None] == ids[None, :],
                                   (tile_end - tiles)[None, :], 0), axis=1)
    t = jnp.arange(n_tiles, dtype=jnp.int32)
    tile_bucket = jnp.sum((t[:, None] >= tile_end[None, :]).astype(jnp.int32), axis=1)
    tile_bucket = jnp.minimum(tile_bucket, N_BUCKETS - 1)
    lo, hi = _bucket_experts()
    return (first_tile * tm + rank, jnp.take(lo, tile_bucket), jnp.take(hi, tile_bucket),
            tile_end[-1:].astype(jnp.int32))


def _row_copy(src, src_row, dst, dst_row, sem):
    return pltpu.make_async_copy(
        src.at[pl.ds(pl.multiple_of(src_row * SUBROWS, SUBROWS), SUBROWS), :],
        dst.at[pl.ds(pl.multiple_of(dst_row * SUBROWS, SUBROWS), SUBROWS), :], sem)


def _dispatch_kernel(pos_ref, h_ref, g_ref, xs_in, xs_out, stage, sem, *, tm):
    del xs_in
    i = pl.program_id(0)
    n = pl.num_programs(0)
    slot = i % 2

    def drain(s):
        pltpu.make_async_copy(stage.at[s], xs_out.at[pl.ds(0, tm * SUBROWS), :],
                              sem.at[s]).wait()

    @pl.when(i >= 2)
    def _():
        drain(slot)

    x = h_ref[...]
    buf = stage.at[slot]
    _to_rows(buf, _bits(x * _rms_scale(x) * g_ref[...]), tm)

    def issue(jj, _):
        for u in range(DMA_UNROLL):
            j = jj * DMA_UNROLL + u
            _row_copy(buf, j, xs_out, pos_ref[i * tm + j], sem.at[slot]).start()
        return 0

    lax.fori_loop(0, tm // DMA_UNROLL, issue, 0)

    @pl.when(i == n - 1)
    def _():
        drain(slot)

        @pl.when(n > 1)
        def _():
            drain(1 - slot)


def _dispatch(h, g, pos, n_rows, *, tm=256):
    N, D = h.shape
    grid_spec = pltpu.PrefetchScalarGridSpec(
        num_scalar_prefetch=1,
        grid=(N // tm,),
        in_specs=[pl.BlockSpec((tm, D), lambda i, pos: (i, 0)),
                  pl.BlockSpec((1, D), lambda i, pos: (0, 0)),
                  pl.BlockSpec(memory_space=pl.ANY)],
        out_specs=pl.BlockSpec(memory_space=pl.ANY),
        scratch_shapes=[pltpu.VMEM((2, tm * SUBROWS, LANES), jnp.uint32),
                        pltpu.SemaphoreType.DMA((2,))],
    )
    return pl.pallas_call(
        functools.partial(_dispatch_kernel, tm=tm),
        grid_spec=grid_spec,
        out_shape=jax.ShapeDtypeStruct((n_rows * SUBROWS, LANES), jnp.uint32),
        input_output_aliases={3: 0},
        compiler_params=_vmem_limit(32),
        name="moe_dispatch",
    )(pos, h, g.reshape(1, D), jnp.zeros((n_rows * SUBROWS, LANES), jnp.uint32))


def _experts_kernel(ta_ref, tb_ref, nu_ref, x_ref, wgu_a, wd_a, wgu_b, wd_b, y_ref, *, tm):
    del ta_ref, tb_ref

    @pl.when(pl.program_id(0) < nu_ref[0])
    def _():
        x = lax.bitcast_convert_type(_from_rows(x_ref, tm), F32).astype(BF16)

        def expert(wgu_ref, wd_ref):
            gu = _dot(x, wgu_ref[0, 0])
            gate = gu[:, :D_EXPERT]
            hmid = gate * jax.nn.sigmoid(gate) * gu[:, D_EXPERT:]
            y = _dot(hmid.astype(BF16), wd_ref[0, 0])
            return _bits(y.astype(BF16).astype(F32))

        _to_rows(y_ref, (expert(wgu_a, wd_a) >> 16) | expert(wgu_b, wd_b), tm)


def _experts(xs, tile_a, tile_b, n_used, w_gu, w_d, layer, *, tm):
    n_tiles = tile_a.shape[0]
    D = D_MODEL
    last = lambda i, nu: jnp.minimum(i, nu[0] - 1)
    rows = pl.BlockSpec((tm * SUBROWS, LANES), lambda i, ta, tb, nu: (last(i, nu), 0))
    gu = lambda which: pl.BlockSpec(
        (1, 1, D, 2 * D_EXPERT),
        lambda i, ta, tb, nu: (layer, (ta, tb)[which][last(i, nu)], 0, 0))
    dn = lambda which: pl.BlockSpec(
        (1, 1, D_EXPERT, D),
        lambda i, ta, tb, nu: (layer, (ta, tb)[which][last(i, nu)], 0, 0))
    grid_spec = pltpu.PrefetchScalarGridSpec(
        num_scalar_prefetch=3,
        grid=(n_tiles,),
        in_specs=[rows, gu(0), dn(0), gu(1), dn(1)],
        out_specs=rows,
    )
    return pl.pallas_call(
        functools.partial(_experts_kernel, tm=tm),
        grid_spec=grid_spec,
        out_shape=jax.ShapeDtypeStruct(xs.shape, jnp.uint32),
        input_output_aliases={3: 0},
        compiler_params=_vmem_limit(40),
        name="moe_experts",
    )(tile_a, tile_b, n_used, xs, w_gu, w_d, w_gu, w_d)


def _moe_rows(h, g, w_rg, w_re, w_gu, w_d, layer):
    N = h.shape[0]
    tm = -(-N // N_BUCKETS // EXPERT_TILE_ALIGN) * EXPERT_TILE_ALIGN
    n_tiles = N // tm + N_BUCKETS
    route, counts = _route(h, g, w_rg, w_re)
    pos, tile_a, tile_b, n_used = _dispatch_plan(route, counts, tm, n_tiles)
    xs = _dispatch(h, g, pos, n_tiles * tm)
    return route, pos, _experts(xs, tile_a, tile_b, n_used, w_gu, w_d, layer, tm=tm)


def _combine_ple(pos_ref, h_ref, route_ref, p_ref, g_ref, wg_ref, wp_ref, ys_hbm, buf, sem, *, tm):
    i = pl.program_id(0)
    n = pl.num_programs(0)
    slot = i % 2

    def wait(s):
        pltpu.make_async_copy(ys_hbm.at[pl.ds(0, tm * SUBROWS), :], buf.at[s], sem.at[s]).wait()

    @pl.when(i == 0)
    def _():
        def issue(jj, _):
            for u in range(DMA_UNROLL):
                j = jj * DMA_UNROLL + u
                _row_copy(ys_hbm, pos_ref[j], buf.at[0], j, sem.at[0]).start()
            return 0
        lax.fori_loop(0, tm // DMA_UNROLL, issue, 0)

    rows = buf.at[slot]
    wait(slot)
    nxt = jnp.minimum(i + 1, n - 1) * tm
    for j in range(tm):
        _row_copy(ys_hbm, pos_ref[nxt + j], buf.at[1 - slot], j, sem.at[1 - slot]).start()
    packed = _from_rows(rows, tm)
    y_lo = lax.bitcast_convert_type(packed << 16, F32)
    y_hi = lax.bitcast_convert_type(packed & jnp.uint32(0xFFFF0000), F32)
    r = route_ref[...]
    h = h_ref[...] + r[:, 2:3] * y_lo + r[:, 3:4] * y_hi
    xn = (h * _rms_scale(h) * g_ref[...]).astype(BF16)
    gate = jax.nn.sigmoid(_dot(xn, wg_ref[...]))
    proj = _dot(p_ref[0].astype(BF16), wp_ref[...])
    return h + proj * gate


def _drain_last_fetch(ys_hbm, buf, sem, tm):
    i = pl.program_id(0)

    @pl.when(i == pl.num_programs(0) - 1)
    def _():
        s = 1 - i % 2
        pltpu.make_async_copy(ys_hbm.at[pl.ds(0, tm * SUBROWS), :], buf.at[s], sem.at[s]).wait()


def _ple_qkv_kernel(pos_ref, h_ref, route_ref, p_ref, g_ref, wg_ref, wp_ref, gkv_ref, wkv_ref,
                    gq_ref, wq_ref, ys_hbm, ho_ref, q_ref, kv_ref, buf, sem, *, tm):
    hn = _combine_ple(pos_ref, h_ref, route_ref, p_ref, g_ref, wg_ref, wp_ref, ys_hbm, buf, sem,
                      tm=tm)
    ho_ref[...] = hn
    base = hn * _rms_scale(hn)
    kv_ref[...] = _dot((base * gkv_ref[...]).astype(BF16), wkv_ref[...]).astype(BF16)
    q = _dot((base * gq_ref[...]).astype(BF16), wq_ref[...])
    q_ref[...] = (q * QUERY_SCALE).astype(BF16)
    _drain_last_fetch(ys_hbm, buf, sem, tm)


def _ple_final_kernel(pos_ref, h_ref, route_ref, p_ref, g_ref, wg_ref, wp_ref, gf_ref, ys_hbm,
                      o_ref, buf, sem, *, tm):
    hn = _combine_ple(pos_ref, h_ref, route_ref, p_ref, g_ref, wg_ref, wp_ref, ys_hbm, buf, sem,
                      tm=tm)
    o_ref[...] = hn * _rms_scale(hn) * gf_ref[...]
    _drain_last_fetch(ys_hbm, buf, sem, tm)


def _ple_call(body, name, pos, h, route, p, layer, const_inputs, ys, out_widths, out_dtypes, *,
              tm=512):
    N = h.shape[0]
    tok = lambda w: pl.BlockSpec((tm, w), lambda i, pos: (i, 0))
    const = lambda a: pl.BlockSpec(a.shape, lambda i, pos: (0, 0))
    grid_spec = pltpu.PrefetchScalarGridSpec(
        num_scalar_prefetch=1,
        grid=(N // tm,),
        in_specs=([tok(h.shape[1]), tok(route.shape[1]),
                   pl.BlockSpec((1, tm, p.shape[2]), lambda i, pos: (layer, i, 0))]
                  + [const(a) for a in const_inputs] + [pl.BlockSpec(memory_space=pl.ANY)]),
        out_specs=[tok(w) for w in out_widths],
        scratch_shapes=[pltpu.VMEM((2, tm * SUBROWS, LANES), jnp.uint32),
                        pltpu.SemaphoreType.DMA((2,))],
    )
    return pl.pallas_call(
        functools.partial(body, tm=tm),
        grid_spec=grid_spec,
        out_shape=[jax.ShapeDtypeStruct((N, w), dt) for w, dt in zip(out_widths, out_dtypes)],
        compiler_params=_vmem_limit(48),
        name=name,
    )(pos, h, route, p, *const_inputs, ys)


def _ple_qkv(h, route, pos, ys, p, layer, g_ple, w_gate, w_proj, g_kv, w_kv, g_q, w_q):
    D = h.shape[1]
    row = lambda v: v.reshape(1, D)
    consts = [row(g_ple), w_gate.astype(BF16), w_proj.astype(BF16),
              row(g_kv), w_kv.astype(BF16), row(g_q), w_q.astype(BF16)]
    return _ple_call(_ple_qkv_kernel, "ple_qkv", pos, h, route, p, layer, consts, ys,
                     (D, D, 2 * D), (F32, BF16, BF16))


def _ple_final(h, route, pos, ys, p, layer, g_ple, w_gate, w_proj, g_final):
    D = h.shape[1]
    row = lambda v: v.reshape(1, D)
    consts = [row(g_ple), w_gate.astype(BF16), w_proj.astype(BF16), row(g_final)]
    return _ple_call(_ple_final_kernel, "ple_final", pos, h, route, p, layer, consts, ys,
                     (D,), (F32,))[0]


def _attn_kernel(q_ref, k_ref, v_ref, o_ref, *, tq, win, chunk, unroll):
    c = pl.program_id(2)
    nrow = 2 * tq
    lane = lax.broadcasted_iota(jnp.int32, (1, LANES), 1)
    even = lane < HEAD_DIM
    rows = lax.broadcasted_iota(jnp.int32, (nrow, win), 0)
    cols = lax.broadcasted_iota(jnp.int32, (nrow, win), 1)
    diag = cols - jnp.where(rows >= tq, rows - tq, rows)
    kj = lax.broadcasted_iota(jnp.int32, (win, win), 0)
    ks = lax.broadcasted_iota(jnp.int32, (win, win), 1)
    tri = (kj > ks).astype(BF16)

    def windows(q2s, wss, valids, carries):
        n = len(q2s)
        wss = [pl.multiple_of(ws, tq) for ws in wss]
        zs = [lax.dot_general(q2s[u], k_ref[0, pl.ds(wss[u], win), :], (((1,), (1,)), ((), ())),
                              preferred_element_type=F32) for u in range(n)]
        zs = [jnp.where(valid, z, MASKED_SCORE) for valid, z in zip(valids, zs)]
        sps = [jnp.maximum(z, 0.0) + jnp.log2(1.0 + jnp.exp2(-jnp.abs(z))) for z in zs]
        logsigs = [z - sp for z, sp in zip(zs, sps)]
        suffixes = [_dot(sp.astype(BF16), tri) for sp in sps]
        if carries is not None:
            suffixes = [s + carry for s, carry in zip(suffixes, carries)]
        probs = [jnp.exp2(ls - s) for ls, s in zip(logsigs, suffixes)]
        outs = [_dot(probs[u].astype(BF16), v_ref[0, pl.ds(wss[u], win), :]) for u in range(n)]
        return outs, [s[:, :1] + sp[:, :1] for s, sp in zip(suffixes, sps)]

    def group(ig, _):
        q2s, ws0, valids = [], [], []
        for u in range(unroll):
            r0 = pl.multiple_of((ig * unroll + u) * tq, tq)
            q0 = c * chunk + r0
            qb = q_ref[0, pl.ds(r0, tq), :]
            zero = jnp.zeros_like(qb)
            q2s.append(jnp.concatenate([jnp.where(even, qb, zero), jnp.where(even, zero, qb)],
                                       axis=0))
            ws0.append(jnp.maximum(q0 + tq - win, 0))
            valids.append(diag < q0 - ws0[u])
        outs, totals = windows(q2s, ws0, valids, None)

        def least_mass(ts):
            return jnp.min(functools.reduce(jnp.minimum, ts))

        def cond(st):
            w, low, _, _ = st
            return jnp.logical_and(ws0[-1] - (w - 1) * win > 0, low < SUFFIX_CUTOFF)

        def body(st):
            w, _, outs, totals = st
            uppers = [jnp.maximum(ws - (w - 1) * win, 0) for ws in ws0]
            wss = [jnp.maximum(upper - win, 0) for upper in uppers]
            more, new_t = windows(q2s, wss, [cols < upper - ws for upper, ws in zip(uppers, wss)],
                                  totals)
            return (w + 1, least_mass(new_t), tuple(o + m for o, m in zip(outs, more)),
                    tuple(new_t))

        _, _, outs, _ = lax.while_loop(
            cond, body, (jnp.int32(1), least_mass(totals), tuple(outs), tuple(totals)))
        for u in range(unroll):
            r0 = pl.multiple_of((ig * unroll + u) * tq, tq)
            o = outs[u]
            o_ref[0, pl.ds(r0, tq), :] = jnp.where(even, o[:tq], o[tq:]).astype(o_ref.dtype)
        return 0

    lax.fori_loop(0, chunk // (tq * unroll), group, 0)


def _attention(q, kv, *, tq=64, win=256, chunk=1024, unroll=8):
    B, S, D = q.shape
    n_pairs = D // LANES
    return pl.pallas_call(
        functools.partial(_attn_kernel, tq=tq, win=win, chunk=chunk, unroll=unroll),
        grid=(B, n_pairs, S // chunk),
        in_specs=[
            pl.BlockSpec((1, chunk, LANES), lambda b, hp, c: (b, c, hp)),
            pl.BlockSpec((1, S, LANES), lambda b, hp, c: (b, 0, hp)),
            pl.BlockSpec((1, S, LANES), lambda b, hp, c: (b, 0, n_pairs + hp)),
        ],
        out_specs=pl.BlockSpec((1, chunk, LANES), lambda b, hp, c: (b, c, hp)),
        out_shape=jax.ShapeDtypeStruct((B, S, D), BF16),
        compiler_params=_vmem_limit(48),
        name="stickbreak_attn",
    )(q, kv, kv)


def _oproj_kernel(h_ref, o_ref, w_ref, out_ref):
    out_ref[...] = h_ref[...] + _dot(o_ref[...], w_ref[...])


def _oproj(h, o, w_o, *, tm=1024):
    N, D = h.shape
    tok = pl.BlockSpec((tm, D), lambda i: (i, 0))
    return pl.pallas_call(
        _oproj_kernel,
        grid=(N // tm,),
        in_specs=[tok, tok, pl.BlockSpec((D, D), lambda i: (0, 0))],
        out_specs=tok,
        out_shape=jax.ShapeDtypeStruct((N, D), F32),
        compiler_params=_vmem_limit(40),
        name="attn_oproj",
    )(h, o, w_o.astype(BF16))


def kernel(x, p, g_mix_a, w_pool, b_pool, ls_pool, g_kv, w_kv, g_mix_b, w_q, w_o, g_moe, w_router_group, w_router_expert, w_gate_up, w_down, g_ple, w_ple_gate, w_ple_proj, g_final):
    B, S, D = x.shape
    N = B * S
    assert p.shape[0] == 2 and g_mix_a.shape[0] == 1 and g_mix_b.shape[0] == 1
    p = p.reshape(p.shape[0], N, PLE_DIM)
    w_gu = w_gate_up.astype(BF16)
    w_dn = w_down.astype(BF16)

    h = _pool_mixer(x, g_mix_a[0], w_pool[0], b_pool[0].reshape(-1), ls_pool[0]).reshape(N, D)
    route, pos, ys = _moe_rows(h, g_moe[0], w_router_group[0], w_router_expert[0], w_gu, w_dn, 0)
    h, q, kv = _ple_qkv(h, route, pos, ys, p, 0, g_ple[0], w_ple_gate[0], w_ple_proj[0],
                        g_kv, w_kv, g_mix_b[0], w_q[0])

    o = _attention(q.reshape(B, S, D), kv.reshape(B, S, 2 * D)).reshape(N, D)
    h = _oproj(h, o, w_o[0])
    route, pos, ys = _moe_rows(h, g_moe[1], w_router_group[1], w_router_expert[1], w_gu, w_dn, 1)
    out = _ple_final(h, route, pos, ys, p, 1, g_ple[1], w_ple_gate[1], w_ple_proj[1], g_final)
    return out.reshape(B, S, D)
```

```python
import functools
import math

import jax
import jax.numpy as jnp
from jax import lax
from jax.experimental import pallas as pl
from jax.experimental.pallas import tpu as pltpu

D_MODEL = 1024
POOL_WINDOWS = (2, 4, 8, 16)
POOL_GROUP_DIM = D_MODEL // len(POOL_WINDOWS)
POOL_HALO = 16
HEAD_DIM = 64
N_GROUPS = 4
EXPERTS_PER_GROUP = 8
N_EXPERTS = N_GROUPS * EXPERTS_PER_GROUP
PAIRS_PER_GROUP = EXPERTS_PER_GROUP * (EXPERTS_PER_GROUP - 1) // 2
N_BUCKETS = N_GROUPS * PAIRS_PER_GROUP
D_EXPERT = D_MODEL // 4
PLE_DIM = 256
RMS_EPS = 1e-6
LANES = 128
ROUTER_LANES = LANES
SUBROWS = D_MODEL // LANES
DMA_UNROLL = 8
EXPERT_TILE_ALIGN = 64
QUERY_SCALE = math.log2(math.e) / math.sqrt(HEAD_DIM)
SUFFIX_CUTOFF = 153.0
MASKED_SCORE = -1e30

F32 = jnp.float32
BF16 = jnp.bfloat16

assert N_BUCKETS <= ROUTER_LANES


def _vmem_limit(mib):
    return pltpu.CompilerParams(vmem_limit_bytes=mib * 1024 * 1024)


def _rms_scale(x):
    return lax.rsqrt(jnp.mean(x * x, axis=-1, keepdims=True) + RMS_EPS)


def _split_bf16(x):
    hi = x.astype(BF16)
    lo = (x - hi.astype(F32)).astype(BF16)
    return hi, lo


def _dot(a, b):
    return jnp.dot(a, b, preferred_element_type=F32)


def _pool_kernel(x_ref, xprev_ref, g_ref, w_ref, b_ref, ls_ref, o_ref, *, ts):
    i = pl.program_id(1)
    xc = x_ref[0]
    xp = xprev_ref[0]
    g = g_ref[...]
    xn_c = xc * _rms_scale(xc) * g
    xn_p = xp * _rms_scale(xp) * g * (i > 0).astype(F32)
    full = jnp.concatenate([xn_p, xn_c], axis=0)
    t_glob = i * ts + lax.broadcasted_iota(jnp.int32, (ts, 1), 0)
    ys = []
    for gi, w in enumerate(POOL_WINDOWS):
        f = full[:, gi * POOL_GROUP_DIM:(gi + 1) * POOL_GROUP_DIM]
        s = f
        k = 1
        while k < w:
            s = s + pltpu.roll(s, k, axis=0)
            k *= 2
        cnt = jnp.minimum(t_glob + 1, w).astype(F32)
        pooled = s[POOL_HALO:] / cnt - f[POOL_HALO:]
        ys.append(_dot(pooled.astype(BF16), w_ref[gi]))
    y = jnp.concatenate(ys, axis=1) + b_ref[...]
    o_ref[0] = xc + y * ls_ref[...]


def _pool_mixer(x, g, w_pool, b_pool, ls_pool, *, ts=512):
    B, S, D = x.shape
    row = lambda v: v.reshape(1, D)
    const = lambda shape: pl.BlockSpec(shape, lambda b, i: (0,) * len(shape))
    return pl.pallas_call(
        functools.partial(_pool_kernel, ts=ts),
        grid=(B, S // ts),
        in_specs=[
            pl.BlockSpec((1, ts, D), lambda b, i: (b, i, 0)),
            pl.BlockSpec((1, POOL_HALO, D),
                         lambda b, i: (b, jnp.maximum(i * (ts // POOL_HALO) - 1, 0), 0)),
            const((1, D)),
            const(w_pool.shape),
            const((1, D)),
            const((1, D)),
        ],
        out_specs=pl.BlockSpec((1, ts, D), lambda b, i: (b, i, 0)),
        out_shape=jax.ShapeDtypeStruct((B, S, D), F32),
        compiler_params=_vmem_limit(40),
        name="pool_mixer",
    )(x, x, row(g), w_pool.astype(BF16), row(b_pool), row(ls_pool))


def _to_rows(ref, x, n):
    for c in range(SUBROWS):
        ref[pl.ds(c, n, stride=SUBROWS), :] = x[:, c * LANES:(c + 1) * LANES]


def _from_rows(ref, n):
    return jnp.concatenate(
        [ref[pl.ds(c, n, stride=SUBROWS), :] for c in range(SUBROWS)], axis=1)


def _bits(x):
    return lax.bitcast_convert_type(x, jnp.uint32)


def _top2(logits):
    t = logits.shape[0]
    lane = lax.broadcasted_iota(jnp.int32, (t, ROUTER_LANES), 1).astype(F32)
    neg = -jnp.inf
    big = float(ROUTER_LANES)
    gl = jnp.where(lane < N_GROUPS, logits, neg)
    gmax = jnp.max(gl, axis=1, keepdims=True)
    g_top_p = 1.0 / jnp.sum(jnp.exp(gl - gmax), axis=1, keepdims=True)
    g_idx = jnp.min(jnp.where(gl == gmax, lane, big), axis=1, keepdims=True)
    lo = N_GROUPS + EXPERTS_PER_GROUP * g_idx
    el = jnp.where((lane >= lo) & (lane < lo + EXPERTS_PER_GROUP), logits, neg)
    v1 = jnp.max(el, axis=1, keepdims=True)
    i1 = jnp.min(jnp.where(el == v1, lane, big), axis=1, keepdims=True)
    el2 = jnp.where(lane == i1, neg, el)
    v2 = jnp.max(el2, axis=1, keepdims=True)
    i2 = jnp.min(jnp.where(el2 == v2, lane, big), axis=1, keepdims=True)
    ex = jnp.exp(v2 - v1)
    return g_idx, i1 - N_GROUPS, i2 - N_GROUPS, g_top_p / (1.0 + ex), g_top_p * ex / (1.0 + ex)


def _route_kernel(h_ref, g_ref, wrh_ref, wrl_ref, route_ref, cnt_ref, run_ref, *, tm):
    @pl.when(pl.program_id(0) == 0)
    def _():
        run_ref[...] = jnp.zeros_like(run_ref)

    x = h_ref[...]
    xh, xl = _split_bf16(x * _rms_scale(x) * g_ref[...])
    wrh = wrh_ref[...]
    grp, e1, e2, w1, w2 = _top2(_dot(xh, wrh) + _dot(xl, wrh) + _dot(xh, wrl_ref[...]))
    lo = jnp.minimum(e1, e2) - EXPERTS_PER_GROUP * grp
    hi = jnp.maximum(e1, e2) - EXPERTS_PER_GROUP * grp
    pair = lo * (2 * EXPERTS_PER_GROUP - 1 - lo) * 0.5 + (hi - lo - 1.0)
    bucket = PAIRS_PER_GROUP * grp + pair
    lane = lax.broadcasted_iota(jnp.int32, (tm, ROUTER_LANES), 1).astype(F32)
    mine = lane == bucket
    onehot = jnp.where(mine, 1.0, 0.0)
    earlier = (lax.broadcasted_iota(jnp.int32, (tm, tm), 0)
               > lax.broadcasted_iota(jnp.int32, (tm, tm), 1)).astype(BF16)
    before = _dot(earlier, onehot.astype(BF16)) + run_ref[...]
    rank = jnp.sum(jnp.where(mine, before, 0.0), axis=1, keepdims=True)
    run_ref[...] += jnp.sum(onehot, axis=0, keepdims=True)
    cnt_ref[...] = run_ref[...]
    first = e1 < e2
    fields = (bucket, rank, jnp.where(first, w1, w2), jnp.where(first, w2, w1))
    out = jnp.zeros((tm, ROUTER_LANES), F32)
    for k, f in enumerate(fields):
        out = jnp.where(lane == k, f, out)
    route_ref[...] = out


def _route(h, g, w_rg, w_re, *, tm=512):
    N, D = h.shape
    w_r = jnp.concatenate([w_rg, w_re], axis=1)
    w_r = jnp.pad(w_r, ((0, 0), (0, ROUTER_LANES - w_r.shape[1])))
    wrh, wrl = _split_bf16(w_r)
    const = lambda a, b: pl.BlockSpec((a, b), lambda i: (0, 0))
    return pl.pallas_call(
        functools.partial(_route_kernel, tm=tm),
        grid=(N // tm,),
        in_specs=[pl.BlockSpec((tm, D), lambda i: (i, 0)), const(1, D),
                  const(D, ROUTER_LANES), const(D, ROUTER_LANES)],
        out_specs=[pl.BlockSpec((tm, ROUTER_LANES), lambda i: (i, 0)), const(1, ROUTER_LANES)],
        out_shape=[jax.ShapeDtypeStruct((N, ROUTER_LANES), F32),
                   jax.ShapeDtypeStruct((1, ROUTER_LANES), F32)],
        scratch_shapes=[pltpu.VMEM((1, ROUTER_LANES), F32)],
        compiler_params=_vmem_limit(32),
        name="moe_route",
    )(h, g.reshape(1, D), wrh, wrl)


def _bucket_experts():
    lo, hi = [], []
    for g in range(N_GROUPS):
        for a in range(EXPERTS_PER_GROUP):
            for b in range(a + 1, EXPERTS_PER_GROUP):
                lo.append(g * EXPERTS_PER_GROUP + a)
                hi.append(g * EXPERTS_PER_GROUP + b)
    return jnp.array(lo, jnp.int32), jnp.array(hi, jnp.int32)


def _dispatch_plan(route, counts, tm, n_tiles):
    bucket = route[:, 0].astype(jnp.int32)
    rank = route[:, 1].astype(jnp.int32)
    cnt = counts[0, :N_BUCKETS].astype(jnp.int32)
    tiles = (cnt + tm - 1) // tm
    tile_end = jnp.cumsum(tiles)
    ids = jnp.arange(N_BUCKETS, dtype=jnp.int32)
    first_tile = jnp.sum(jnp.where(bucket[:, None] == ids[None, :],
                                   (tile_end - tiles)[None, :], 0), axis=1)
    t = jnp.arange(n_tiles, dtype=jnp.int32)
    tile_bucket = jnp.sum((t[:, None] >= tile_end[None, :]).astype(jnp.int32), axis=1)
    tile_bucket = jnp.minimum(tile_bucket, N_BUCKETS - 1)
    lo, hi = _bucket_experts()
    return (first_tile * tm + rank, jnp.take(lo, tile_bucket), jnp.take(hi, tile_bucket),
            tile_end[-1:].astype(jnp.int32))


def _row_copy(src, src_row, dst, dst_row, sem):
    return pltpu.make_async_copy(
        src.at[pl.ds(pl.multiple_of(src_row * SUBROWS, SUBROWS), SUBROWS), :],
        dst.at[pl.ds(pl.multiple_of(dst_row * SUBROWS, SUBROWS), SUBROWS), :], sem)


def _dispatch_kernel(pos_ref, h_ref, g_ref, xs_in, xs_out, stage, sem, *, tm):
    del xs_in
    i = pl.program_id(0)
    n = pl.num_programs(0)
    slot = i % 2

    def drain(s):
        pltpu.make_async_copy(stage.at[s], xs_out.at[pl.ds(0, tm * SUBROWS), :],
                              sem.at[s]).wait()

    @pl.when(i >= 2)
    def _():
        drain(slot)

    x = h_ref[...]
    buf = stage.at[slot]
    _to_rows(buf, _bits(x * _rms_scale(x) * g_ref[...]), tm)

    def issue(jj, _):
        for u in range(DMA_UNROLL):
            j = jj * DMA_UNROLL + u
            _row_copy(buf, j, xs_out, pos_ref[i * tm + j], sem.at[slot]).start()
        return 0

    lax.fori_loop(0, tm // DMA_UNROLL, issue, 0)

    @pl.when(i == n - 1)
    def _():
        drain(slot)

        @pl.when(n > 1)
        def _():
            drain(1 - slot)


def _dispatch(h, g, pos, n_rows, *, tm=512):
    N, D = h.shape
    grid_spec = pltpu.PrefetchScalarGridSpec(
        num_scalar_prefetch=1,
        grid=(N // tm,),
        in_specs=[pl.BlockSpec((tm, D), lambda i, pos: (i, 0)),
                  pl.BlockSpec((1, D), lambda i, pos: (0, 0)),
                  pl.BlockSpec(memory_space=pl.ANY)],
        out_specs=pl.BlockSpec(memory_space=pl.ANY),
        scratch_shapes=[pltpu.VMEM((2, tm * SUBROWS, LANES), jnp.uint32),
                        pltpu.SemaphoreType.DMA((2,))],
    )
    return pl.pallas_call(
        functools.partial(_dispatch_kernel, tm=tm),
        grid_spec=grid_spec,
        out_shape=jax.ShapeDtypeStruct((n_rows * SUBROWS, LANES), jnp.uint32),
        input_output_aliases={3: 0},
        compiler_params=_vmem_limit(32),
        name="moe_dispatch",
    )(pos, h, g.reshape(1, D), jnp.zeros((n_rows * SUBROWS, LANES), jnp.uint32))


def _experts_kernel(ta_ref, tb_ref, nu_ref, x_ref, wgu_a, wd_a, wgu_b, wd_b, y_ref, *, tm):
    del ta_ref, tb_ref

    @pl.when(pl.program_id(0) < nu_ref[0])
    def _():
        x = lax.bitcast_convert_type(_from_rows(x_ref, tm), F32).astype(BF16)

        def expert(wgu_ref, wd_ref):
            gu = _dot(x, wgu_ref[0, 0])
            gate = gu[:, :D_EXPERT]
            hmid = gate * jax.nn.sigmoid(gate) * gu[:, D_EXPERT:]
            y = _dot(hmid.astype(BF16), wd_ref[0, 0])
            return _bits(y.astype(BF16).astype(F32))

        _to_rows(y_ref, (expert(wgu_a, wd_a) >> 16) | expert(wgu_b, wd_b), tm)


def _experts(xs, tile_a, tile_b, n_used, w_gu, w_d, layer, *, tm):
    n_tiles = tile_a.shape[0]
    D = D_MODEL
    last = lambda i, nu: jnp.minimum(i, nu[0] - 1)
    rows = pl.BlockSpec((tm * SUBROWS, LANES), lambda i, ta, tb, nu: (last(i, nu), 0))
    gu = lambda which: pl.BlockSpec(
        (1, 1, D, 2 * D_EXPERT),
        lambda i, ta, tb, nu: (layer, (ta, tb)[which][last(i, nu)], 0, 0))
    dn = lambda which: pl.BlockSpec(
        (1, 1, D_EXPERT, D),
        lambda i, ta, tb, nu: (layer, (ta, tb)[which][last(i, nu)], 0, 0))
    grid_spec = pltpu.PrefetchScalarGridSpec(
        num_scalar_prefetch=3,
        grid=(n_tiles,),
        in_specs=[rows, gu(0), dn(0), gu(1), dn(1)],
        out_specs=rows,
    )
    return pl.pallas_call(
        functools.partial(_experts_kernel, tm=tm),
        grid_spec=grid_spec,
        out_shape=jax.ShapeDtypeStruct(xs.shape, jnp.uint32),
        input_output_aliases={3: 0},
        compiler_params=_vmem_limit(40),
        name="moe_experts",
    )(tile_a, tile_b, n_used, xs, w_gu, w_d, w_gu, w_d)


def _moe_rows(h, g, w_rg, w_re, w_gu, w_d, layer):
    N = h.shape[0]
    tm = -(-N // N_BUCKETS // EXPERT_TILE_ALIGN) * EXPERT_TILE_ALIGN
    n_tiles = N // tm + N_BUCKETS
    route, counts = _route(h, g, w_rg, w_re)
    pos, tile_a, tile_b, n_used = _dispatch_plan(route, counts, tm, n_tiles)
    xs = _dispatch(h, g, pos, n_tiles * tm)
    return route, pos, _experts(xs, tile_a, tile_b, n_used, w_gu, w_d, layer, tm=tm)


def _combine_ple(pos_ref, h_ref, route_ref, p_ref, g_ref, wg_ref, wp_ref, ys_hbm, buf, sem, *, tm):
    i = pl.program_id(0)
    n = pl.num_programs(0)
    slot = i % 2

    def wait(s):
        pltpu.make_async_copy(ys_hbm.at[pl.ds(0, tm * SUBROWS), :], buf.at[s], sem.at[s]).wait()

    @pl.when(i == 0)
    def _():
        def issue(jj, _):
            for u in range(DMA_UNROLL):
                j = jj * DMA_UNROLL + u
                _row_copy(ys_hbm, pos_ref[j], buf.at[0], j, sem.at[0]).start()
            return 0
        lax.fori_loop(0, tm // DMA_UNROLL, issue, 0)

    rows = buf.at[slot]
    wait(slot)
    nxt = jnp.minimum(i + 1, n - 1) * tm
    for j in range(tm):
        _row_copy(ys_hbm, pos_ref[nxt + j], buf.at[1 - slot], j, sem.at[1 - slot]).start()
    packed = _from_rows(rows, tm)
    y_lo = lax.bitcast_convert_type(packed << 16, F32)
    y_hi = lax.bitcast_convert_type(packed & jnp.uint32(0xFFFF0000), F32)
    r = route_ref[...]
    h = h_ref[...] + r[:, 2:3] * y_lo + r[:, 3:4] * y_hi
    xn = (h * _rms_scale(h) * g_ref[...]).astype(BF16)
    gate = jax.nn.sigmoid(_dot(xn, wg_ref[...]))
    proj = _dot(p_ref[0].astype(BF16), wp_ref[...])
    return h + proj * gate


def _drain_last_fetch(ys_hbm, buf, sem, tm):
    i = pl.program_id(0)

    @pl.when(i == pl.num_programs(0) - 1)
    def _():
        s = 1 - i % 2
        pltpu.make_async_copy(ys_hbm.at[pl.ds(0, tm * SUBROWS), :], buf.at[s], sem.at[s]).wait()


def _ple_qkv_kernel(pos_ref, h_ref, route_ref, p_ref, g_ref, wg_ref, wp_ref, gkv_ref, wkv_ref,
                    gq_ref, wq_ref, ys_hbm, ho_ref, q_ref, kv_ref, buf, sem, *, tm):
    hn = _combine_ple(pos_ref, h_ref, route_ref, p_ref, g_ref, wg_ref, wp_ref, ys_hbm, buf, sem,
                      tm=tm)
    ho_ref[...] = hn
    base = hn * _rms_scale(hn)
    kv_ref[...] = _dot((base * gkv_ref[...]).astype(BF16), wkv_ref[...]).astype(BF16)
    q = _dot((base * gq_ref[...]).astype(BF16), wq_ref[...])
    q_ref[...] = (q * QUERY_SCALE).astype(BF16)
    _drain_last_fetch(ys_hbm, buf, sem, tm)


def _ple_final_kernel(pos_ref, h_ref, route_ref, p_ref, g_ref, wg_ref, wp_ref, gf_ref, ys_hbm,
                      o_ref, buf, sem, *, tm):
    hn = _combine_ple(pos_ref, h_ref, route_ref, p_ref, g_ref, wg_ref, wp_ref, ys_hbm, buf, sem,
                      tm=tm)
    o_ref[...] = hn * _rms_scale(hn) * gf_ref[...]
    _drain_last_fetch(ys_hbm, buf, sem, tm)


def _ple_call(body, name, pos, h, route, p, layer, const_inputs, ys, out_widths, out_dtypes, *,
              tm=512):
    N = h.shape[0]
    tok = lambda w: pl.BlockSpec((tm, w), lambda i, pos: (i, 0))
    const = lambda a: pl.BlockSpec(a.shape, lambda i, pos: (0, 0))
    grid_spec = pltpu.PrefetchScalarGridSpec(
        num_scalar_prefetch=1,
        grid=(N // tm,),
        in_specs=([tok(h.shape[1]), tok(route.shape[1]),
                   pl.BlockSpec((1, tm, p.shape[2]), lambda i, pos: (layer, i, 0))]
                  + [const(a) for a in const_inputs] + [pl.BlockSpec(memory_space=pl.ANY)]),
        out_specs=[tok(w) for w in out_widths],
        scratch_shapes=[pltpu.VMEM((2, tm * SUBROWS, LANES), jnp.uint32),
                        pltpu.SemaphoreType.DMA((2,))],
    )
    return pl.pallas_call(
        functools.partial(body, tm=tm),
        grid_spec=grid_spec,
        out_shape=[jax.ShapeDtypeStruct((N, w), dt) for w, dt in zip(out_widths, out_dtypes)],
        compiler_params=_vmem_limit(48),
        name=name,
    )(pos, h, route, p, *const_inputs, ys)


def _ple_qkv(h, route, pos, ys, p, layer, g_ple, w_gate, w_proj, g_kv, w_kv, g_q, w_q):
    D = h.shape[1]
    row = lambda v: v.reshape(1, D)
    consts = [row(g_ple), w_gate.astype(BF16), w_proj.astype(BF16),
              row(g_kv), w_kv.astype(BF16), row(g_q), w_q.astype(BF16)]
    return _ple_call(_ple_qkv_kernel, "ple_qkv", pos, h, route, p, layer, consts, ys,
                     (D, D, 2 * D), (F32, BF16, BF16))


def _ple_final(h, route, pos, ys, p, layer, g_ple, w_gate, w_proj, g_final):
    D = h.shape[1]
    row = lambda v: v.reshape(1, D)
    consts = [row(g_ple), w_gate.astype(BF16), w_proj.astype(BF16), row(g_final)]
    return _ple_call(_ple_final_kernel, "ple_final", pos, h, route, p, layer, consts, ys,
                     (D,), (F32,))[0]


def _attn_kernel(q_ref, k_ref, v_ref, o_ref, *, tq, win, chunk, unroll):
    c = pl.program_id(2)
    nrow = 2 * tq
    lane = lax.broadcasted_iota(jnp.int32, (1, LANES), 1)
    even = lane < HEAD_DIM
    rows = lax.broadcasted_iota(jnp.int32, (nrow, win), 0)
    cols = lax.broadcasted_iota(jnp.int32, (nrow, win), 1)
    diag = cols - jnp.where(rows >= tq, rows - tq, rows)
    kj = lax.broadcasted_iota(jnp.int32, (win, win), 0)
    ks = lax.broadcasted_iota(jnp.int32, (win, win), 1)
    tri = (kj > ks).astype(BF16)

    def windows(q2s, wss, valids, carries):
        n = len(q2s)
        wss = [pl.multiple_of(ws, tq) for ws in wss]
        zs = [lax.dot_general(q2s[u], k_ref[0, pl.ds(wss[u], win), :], (((1,), (1,)), ((), ())),
                              preferred_element_type=F32) for u in range(n)]
        zs = [jnp.where(valid, z, MASKED_SCORE) for valid, z in zip(valids, zs)]
        sps = [jnp.maximum(z, 0.0) + jnp.log2(1.0 + jnp.exp2(-jnp.abs(z))) for z in zs]
        logsigs = [z - sp for z, sp in zip(zs, sps)]
        suffixes = [_dot(sp.astype(BF16), tri) for sp in sps]
        if carries is not None:
            suffixes = [s + carry for s, carry in zip(suffixes, carries)]
        probs = [jnp.exp2(ls - s) for ls, s in zip(logsigs, suffixes)]
        outs = [_dot(probs[u].astype(BF16), v_ref[0, pl.ds(wss[u], win), :]) for u in range(n)]
        return outs, [s[:, :1] + sp[:, :1] for s, sp in zip(suffixes, sps)]

    def group(ig, _):
        q2s, ws0, valids = [], [], []
        for u in range(unroll):
            r0 = pl.multiple_of((ig * unroll + u) * tq, tq)
            q0 = c * chunk + r0
            qb = q_ref[0, pl.ds(r0, tq), :]
            zero = jnp.zeros_like(qb)
            q2s.append(jnp.concatenate([jnp.where(even, qb, zero), jnp.where(even, zero, qb)],
                                       axis=0))
            ws0.append(jnp.maximum(q0 + tq - win, 0))
            valids.append(diag < q0 - ws0[u])
        outs, totals = windows(q2s, ws0, valids, None)

        def least_mass(ts):
            return jnp.min(functools.reduce(jnp.minimum, ts))

        def cond(st):
            w, low, _, _ = st
            return jnp.logical_and(ws0[-1] - (w - 1) * win > 0, low < SUFFIX_CUTOFF)

        def body(st):
            w, _, outs, totals = st
            uppers = [jnp.maximum(ws - (w - 1) * win, 0) for ws in ws0]
            wss = [jnp.maximum(upper - win, 0) for upper in uppers]
            more, new_t = windows(q2s, wss, [cols < upper - ws for upper, ws in zip(uppers, wss)],
                                  totals)
            return (w + 1, least_mass(new_t), tuple(o + m for o, m in zip(outs, more)),
                    tuple(new_t))

        _, _, outs, _ = lax.while_loop(
            cond, body, (jnp.int32(1), least_mass(totals), tuple(outs), tuple(totals)))
        for u in range(unroll):
            r0 = pl.multiple_of((ig * unroll + u) * tq, tq)
            o = outs[u]
            o_ref[0, pl.ds(r0, tq), :] = jnp.where(even, o[:tq], o[tq:]).astype(o_ref.dtype)
        return 0

    lax.fori_loop(0, chunk // (tq * unroll), group, 0)


def _attention(q, kv, *, tq=64, win=256, chunk=2048, unroll=8):
    B, S, D = q.shape
    n_pairs = D // LANES
    return pl.pallas_call(
        functools.partial(_attn_kernel, tq=tq, win=win, chunk=chunk, unroll=unroll),
        grid=(B, n_pairs, S // chunk),
        in_specs=[
            pl.BlockSpec((1, chunk, LANES), lambda b, hp, c: (b, c, hp)),
            pl.BlockSpec((1, S, LANES), lambda b, hp, c: (b, 0, hp)),
            pl.BlockSpec((1, S, LANES), lambda b, hp, c: (b, 0, n_pairs + hp)),
        ],
        out_specs=pl.BlockSpec((1, chunk, LANES), lambda b, hp, c: (b, c, hp)),
        out_shape=jax.ShapeDtypeStruct((B, S, D), BF16),
        compiler_params=_vmem_limit(48),
        name="stickbreak_attn",
    )(q, kv, kv)


def _oproj_kernel(h_ref, o_ref, w_ref, out_ref):
    out_ref[...] = h_ref[...] + _dot(o_ref[...], w_ref[...])


def _oproj(h, o, w_o, *, tm=1024):
    N, D = h.shape
    tok = pl.BlockSpec((tm, D), lambda i: (i, 0))
    return pl.pallas_call(
        _oproj_kernel,
        grid=(N // tm,),
        in_specs=[tok, tok, pl.BlockSpec((D, D), lambda i: (0, 0))],
        out_specs=tok,
        out_shape=jax.ShapeDtypeStruct((N, D), F32),
        compiler_params=_vmem_limit(40),
        name="attn_oproj",
    )(h, o, w_o.astype(BF16))


def kernel(x, p, g_mix_a, w_pool, b_pool, ls_pool, g_kv, w_kv, g_mix_b, w_q, w_o, g_moe, w_router_group, w_router_expert, w_gate_up, w_down, g_ple, w_ple_gate, w_ple_proj, g_final):
    B, S, D = x.shape
    N = B * S
    assert p.shape[0] == 2 and g_mix_a.shape[0] == 1 and g_mix_b.shape[0] == 1
    p = p.reshape(p.shape[0], N, PLE_DIM)
    w_gu = w_gate_up.astype(BF16)
    w_dn = w_down.astype(BF16)

    h = _pool_mixer(x, g_mix_a[0], w_pool[0], b_pool[0].reshape(-1), ls_pool[0]).reshape(N, D)
    route, pos, ys = _moe_rows(h, g_moe[0], w_router_group[0], w_router_expert[0], w_gu, w_dn, 0)
    h, q, kv = _ple_qkv(h, route, pos, ys, p, 0, g_ple[0], w_ple_gate[0], w_ple_proj[0],
                        g_kv, w_kv, g_mix_b[0], w_q[0])

    o = _attention(q.reshape(B, S, D), kv.reshape(B, S, 2 * D)).reshape(N, D)
    h = _oproj(h, o, w_o[0])
    route, pos, ys = _moe_rows(h, g_moe[1], w_router_group[1], w_router_expert[1], w_gu, w_dn, 1)
    out = _ple_final(h, route, pos, ys, p, 1, g_ple[1], w_ple_gate[1], w_ple_proj[1], g_final)
    return out.reshape(B, S, D)
```
